```python
import jax, jax.numpy as jnp
from jax import lax
import numpy as np

D_MODEL = 1024
BATCH = 32
SEQ = 2048
DEPTH = 1
DEC_BATCH = 32
DEC_SEQ = 16
PAST_LEN = 4096

CHUNK = 64
N_META = 16
Q_BLOCK = 128
H_SB = 8
DH_SB = 64
SB_WIDTH = H_SB * DH_SB
H_RET = 4
DK_RET = 128
DV_RET = 256
RET_QK_WIDTH = H_RET * DK_RET
RET_V_WIDTH = H_RET * DV_RET
D_IN = 3 * SB_WIDTH + 2 * RET_QK_WIDTH + 2 * RET_V_WIDTH + 2 * D_MODEL
ROPE_BASE = 10000.0
N_EXPERTS = 32
TOP_K = 4
D_FF = 1024
SWIGLU_LIMIT = 7.0
SWIGLU_ALPHA = 1.702
MOE_BLOCK = 256
EPS = 1e-6

kernel_name = 'stickbreak_retention_moe_stream_step'

F32 = jnp.float32


def rmsnorm(x, g):
    x32 = x.astype(F32)
    y = x32 * lax.rsqrt(jnp.mean(x32 * x32, axis=-1, keepdims=True) + EPS)
    return (y * g.astype(F32)).astype(x.dtype)


def split_projection(p):
    sizes = (SB_WIDTH,) * 3 + (RET_QK_WIDTH,) * 2 + (RET_V_WIDTH,) * 2 + (D_MODEL,) * 2
    idx = np.cumsum(sizes)[:-1].tolist()
    return jnp.split(p, idx, axis=-1)


def stick_breaking(q, k, v, q_pos, k_pos):
    z = jnp.einsum('bqhd,bkhd->bhqk', q, k, preferred_element_type=F32) * (DH_SB ** -0.5)
    mask = k_pos[None, :] < q_pos[:, None]
    log_beta = jax.nn.log_sigmoid(z)
    log_keep = jnp.where(mask, log_beta - z, 0.0)
    between = lax.cumsum(log_keep, axis=3, reverse=True) - log_keep
    w = jnp.where(mask, jnp.exp(log_beta + between), 0.0)
    out = jnp.einsum('bhqk,bkhd->bqhd', w.astype(v.dtype), v, preferred_element_type=F32)
    return out.astype(v.dtype)


def sb_prompt(q, k, v):
    L = q.shape[1]
    pos = jnp.arange(L)
    blocks = []
    for start in range(0, L, Q_BLOCK):
        end = min(start + Q_BLOCK, L)
        blocks.append(stick_breaking(q[:, start:end], k[:, :end], v[:, :end], pos[start:end], pos[:end]))
    return jnp.concatenate(blocks, axis=1)


def rotate(x, pos):
    half = x.shape[-1] // 2
    inv_freq = 1.0 / (ROPE_BASE ** jnp.linspace(0.0, 1.0, half, dtype=F32))
    ang = pos.astype(F32)[:, None] * inv_freq[None, :]
    cos = jnp.cos(ang)[:, None, :]
    sin = jnp.sin(ang)[:, None, :]
    x32 = x.astype(F32)
    x1, x2 = x32[..., :half], x32[..., half:]
    return jnp.concatenate([x1 * cos - x2 * sin, x1 * sin + x2 * cos], axis=-1).astype(x.dtype)


def retention_log_decay():
    return jnp.log(1.0 - 2.0 ** (-5.0 - jnp.arange(H_RET, dtype=F32)))


def retention_chunk(S, q, k, v, log_g):
    q, k, v = q.astype(F32), k.astype(F32), v.astype(F32)
    S = S.astype(F32)
    C = q.shape[2]
    idx = jnp.arange(C, dtype=F32)
    lg = log_g[:, None]
    rel = idx[:, None] - idx[None, :]
    decay = jnp.where(rel >= 0, jnp.exp(lg[:, :, None] * jnp.maximum(rel, 0.0)), 0.0)
    inner = jnp.einsum('bhid,bhjd->bhij', q, k) * decay
    o = (jnp.einsum('bhij,bhjv->bhiv', inner, v)
         + jnp.einsum('bhid,bhdv->bhiv', q, S) * jnp.exp(lg * (idx + 1.0))[:, :, None])
    k_w = k * jnp.exp(lg * (C - 1.0 - idx))[:, :, None]
    S_new = jnp.exp(lg * C)[:, :, None] * S + jnp.einsum('bhjd,bhjv->bhdv', k_w, v)
    return S_new, o


def retention_prompt(q, k, v, log_g):
    B, L = q.shape[:2]
    pad = (-N_META) % CHUNK
    nc = (L + pad) // CHUNK

    def chunks(t):
        t = jnp.pad(t, ((0, 0), (pad, 0), (0, 0), (0, 0)))
        return t.reshape(B, nc, CHUNK, t.shape[2], t.shape[3]).transpose(1, 0, 3, 2, 4)

    S0 = jnp.zeros((B, H_RET, DK_RET, DV_RET), F32)
    S_fin, o = lax.scan(lambda S, qkv: retention_chunk(S, qkv[0], qkv[1], qkv[2], log_g),
                        S0, (chunks(q), chunks(k), chunks(v)))
    o = o.transpose(1, 0, 3, 2, 4).reshape(B, nc * CHUNK, H_RET, DV_RET)[:, pad:]
    return o, S_fin


def retention_output(o, g):
    B, T = g.shape[:2]
    on = o * lax.rsqrt(jnp.mean(o * o, axis=-1, keepdims=True) + EPS)
    return (jax.nn.silu(g.astype(F32)) * on.reshape(B, T, RET_V_WIDTH)).astype(g.dtype)


def moe_ffn(x, w_router, b_router, w_gate_up, b_gate_up, w_down, b_down):
    T, D = x.shape
    logits = jnp.dot(x, w_router, preferred_element_type=F32) + b_router.astype(F32)
    top_logit, top_e = lax.top_k(logits, TOP_K)
    gate = jax.nn.softmax(top_logit, axis=-1)
    M = T * TOP_K
    flat_e = top_e.reshape(M)
    order = jnp.argsort(flat_e)
    sorted_e = flat_e[order]
    sorted_tok = (order // TOP_K).astype(jnp.int32)
    sorted_gate = gate.reshape(M)[order]
    counts = jnp.bincount(flat_e, length=N_EXPERTS)
    padded = (counts + MOE_BLOCK - 1) // MOE_BLOCK * MOE_BLOCK
    pad_end = jnp.cumsum(padded)
    pad_start = pad_end - padded
    grp_start = jnp.cumsum(counts) - counts
    dest = pad_start[sorted_e] + jnp.arange(M) - grp_start[sorted_e]
    nb = -(-M // MOE_BLOCK) + N_EXPERTS
    buf_tok = jnp.zeros(nb * MOE_BLOCK, jnp.int32).at[dest].set(sorted_tok)
    buf_gate = jnp.zeros(nb * MOE_BLOCK, F32).at[dest].set(sorted_gate)
    block_e = jnp.minimum(jnp.searchsorted(pad_end, jnp.arange(nb) * MOE_BLOCK, side='right'), N_EXPERTS - 1)

    def expert_block(args):
        tok, e = args
        xb = x[tok]
        h = jnp.dot(xb, w_gate_up[e]) + b_gate_up[e]
        g = jnp.minimum(h[:, :D_FF], SWIGLU_LIMIT)
        u = jnp.clip(h[:, D_FF:], -SWIGLU_LIMIT, SWIGLU_LIMIT)
        act = (u + 1.0) * (g * jax.nn.sigmoid(SWIGLU_ALPHA * g))
        return jnp.dot(act, w_down[e]) + b_down[e]

    out = lax.map(expert_block, (buf_tok.reshape(nb, MOE_BLOCK), block_e))
    y = jnp.zeros((T, D), F32).at[buf_tok].add(out.reshape(nb * MOE_BLOCK, D).astype(F32) * buf_gate[:, None])
    return y.astype(x.dtype)


def setup_inputs(seed: int = 0) -> dict:
    key = jax.random.key(seed)
    ks = jax.random.split(key, 19)

    def nrm(k, shape, scale):
        return scale * jax.random.normal(k, shape, F32)

    return {
        'x_prompt': nrm(ks[0], (BATCH, SEQ, D_MODEL), 1.0),
        'x_sample': nrm(ks[1], (DEC_BATCH, DEC_SEQ, D_MODEL), 1.0),
        'cache_sb_k': nrm(ks[2], (DEPTH, DEC_BATCH, PAST_LEN, H_SB, DH_SB), 1.0),
        'cache_sb_v': nrm(ks[3], (DEPTH, DEC_BATCH, PAST_LEN, H_SB, DH_SB), 1.0),
        'state_ret': nrm(ks[4], (DEPTH, DEC_BATCH, H_RET, DK_RET, DV_RET), 0.5),
        'meta_tokens': nrm(ks[5], (N_META, D_MODEL), 1.0),
        'norm_mix': 1.0 + nrm(ks[6], (DEPTH, D_MODEL), 0.02),
        'w_in': nrm(ks[7], (DEPTH, D_MODEL, D_IN), D_MODEL ** -0.5),
        'w_branch_sb': nrm(ks[8], (DEPTH, SB_WIDTH, D_MODEL), SB_WIDTH ** -0.5),
        'w_branch_ret': nrm(ks[9], (DEPTH, RET_V_WIDTH, D_MODEL), RET_V_WIDTH ** -0.5),
        'w_out': nrm(ks[10], (DEPTH, D_MODEL, D_MODEL), D_MODEL ** -0.5),
        'norm_ffn': 1.0 + nrm(ks[11], (DEPTH, D_MODEL), 0.02),
        'w_router': nrm(ks[12], (DEPTH, D_MODEL, N_EXPERTS), D_MODEL ** -0.5),
        'b_router': nrm(ks[13], (DEPTH, N_EXPERTS), 0.01),
        'w_gate_up': nrm(ks[14], (DEPTH, N_EXPERTS, D_MODEL, 2 * D_FF), D_MODEL ** -0.5),
        'b_gate_up': nrm(ks[15], (DEPTH, N_EXPERTS, 2 * D_FF), 0.01),
        'w_down': nrm(ks[16], (DEPTH, N_EXPERTS, D_FF, D_MODEL), D_FF ** -0.5),
        'b_down': nrm(ks[17], (DEPTH, N_EXPERTS, D_MODEL), 0.01),
        'norm_final': 1.0 + nrm(ks[18], (D_MODEL,), 0.02),
    }


def reference(x_prompt, x_sample, cache_sb_k, cache_sb_v, state_ret, meta_tokens, norm_mix, w_in,
              w_branch_sb, w_branch_ret, w_out, norm_ffn, w_router, b_router, w_gate_up, b_gate_up,
              w_down, b_down, norm_final):
    B, S_len, D = x_prompt.shape
    Bs, Ts, _ = x_sample.shape
    past = cache_sb_k.shape[2]
    L = N_META + S_len
    log_g = retention_log_decay()
    pos_ret_p = jnp.arange(L) - N_META
    pos_ret_s = past + jnp.arange(Ts)
    sb_q_pos_s = N_META + past + jnp.arange(Ts)
    sb_k_pos_s = jnp.arange(N_META + past + Ts)

    hp = jnp.concatenate([jnp.broadcast_to(meta_tokens.astype(x_prompt.dtype)[None], (B, N_META, D)), x_prompt], axis=1)
    hs = x_sample
    kp_l, vp_l, sp_l, ks_l, vs_l, ss_l = [], [], [], [], [], []
    for l in range(DEPTH):
        n_p = rmsnorm(hp, norm_mix[l])
        qa, ka, va, qr, kr, vr, gr, ga, gb = split_projection(n_p @ w_in[l])
        qa, ka, va = (t.reshape(B, L, H_SB, DH_SB) for t in (qa, ka, va))
        ya = sb_prompt(qa, ka, va).reshape(B, L, SB_WIDTH)
        qr = rotate(qr.reshape(B, L, H_RET, DK_RET), pos_ret_p)
        kr = rotate(kr.reshape(B, L, H_RET, DK_RET), pos_ret_p) * (DK_RET ** -0.5)
        o_r, s_p = retention_prompt(qr, kr, vr.reshape(B, L, H_RET, DV_RET), log_g)
        yr = retention_output(o_r, gr)
        mix_p = jax.nn.sigmoid(ga) * (ya @ w_branch_sb[l]) + jax.nn.sigmoid(gb) * (yr @ w_branch_ret[l])
        hp = hp + mix_p @ w_out[l]
        kp_l.append(ka)
        vp_l.append(va)
        sp_l.append(s_p)

        n_s = rmsnorm(hs, norm_mix[l])
        qa_s, ka_s, va_s, qr_s, kr_s, vr_s, gr_s, ga_s, gb_s = split_projection(n_s @ w_in[l])
        qa_s, ka_s, va_s = (t.reshape(Bs, Ts, H_SB, DH_SB) for t in (qa_s, ka_s, va_s))
        meta_k = jnp.broadcast_to(ka[0, :N_META][None], (Bs, N_META, H_SB, DH_SB))
        meta_v = jnp.broadcast_to(va[0, :N_META][None], (Bs, N_META, H_SB, DH_SB))
        k_all = jnp.concatenate([meta_k, cache_sb_k[l].astype(ka_s.dtype), ka_s], axis=1)
        v_all = jnp.concatenate([meta_v, cache_sb_v[l].astype(va_s.dtype), va_s], axis=1)
        ya_s = stick_breaking(qa_s, k_all, v_all, sb_q_pos_s, sb_k_pos_s).reshape(Bs, Ts, SB_WIDTH)
        qr_s = rotate(qr_s.reshape(Bs, Ts, H_RET, DK_RET), pos_ret_s)
        kr_s = rotate(kr_s.reshape(Bs, Ts, H_RET, DK_RET), pos_ret_s) * (DK_RET ** -0.5)
        vr_s = vr_s.reshape(Bs, Ts, H_RET, DV_RET)
        s_s, o_s = retention_chunk(state_ret[l], qr_s.transpose(0, 2, 1, 3), kr_s.transpose(0, 2, 1, 3),
                                   vr_s.transpose(0, 2, 1, 3), log_g)
        yr_s = retention_output(o_s.transpose(0, 2, 1, 3), gr_s)
        mix_s = jax.nn.sigmoid(ga_s) * (ya_s @ w_branch_sb[l]) + jax.nn.sigmoid(gb_s) * (yr_s @ w_branch_ret[l])
        hs = hs + mix_s @ w_out[l]
        ks_l.append(ka_s)
        vs_l.append(va_s)
        ss_l.append(s_s)

        tok = jnp.concatenate([rmsnorm(hp, norm_ffn[l]).reshape(B * L, D),
                               rmsnorm(hs, norm_ffn[l]).reshape(Bs * Ts, D)], axis=0)
        f = moe_ffn(tok, w_router[l], b_router[l], w_gate_up[l], b_gate_up[l], w_down[l], b_down[l])
        hp = hp + f[:B * L].reshape(B, L, D)
        hs = hs + f[B * L:].reshape(Bs, Ts, D)

    y_prompt = rmsnorm(hp[:, N_META:], norm_final)
    y_sample = rmsnorm(hs, norm_final)
    sb_k_prompt = jnp.stack(kp_l)
    sb_v_prompt = jnp.stack(vp_l)
    ret_state_prompt = jnp.stack(sp_l)
    sb_k_sample = jnp.stack(ks_l)
    sb_v_sample = jnp.stack(vs_l)
    ret_state_sample = jnp.stack(ss_l)
    return (y_prompt, y_sample, sb_k_prompt, sb_v_prompt, ret_state_prompt, sb_k_sample, sb_v_sample, ret_state_sample)
```

```python
import functools

import jax
import jax.numpy as jnp
from jax import lax
from jax.experimental import pallas as pl
from jax.experimental.pallas import tpu as pltpu

F32 = jnp.float32
BF16 = jnp.bfloat16

EPS = 1e-6
ROPE_BASE = 10000.0
TOP_K = 4
SWIGLU_LIMIT = 7.0
SWIGLU_ALPHA = 1.702

LANES = 128
VMEM_LIMIT = 56 << 20

_NT = (((1,), (1,)), ((), ()))
_TN = (((0,), (0,)), ((), ()))


def _params(*sem):
    return pltpu.CompilerParams(dimension_semantics=sem, vmem_limit_bytes=VMEM_LIMIT)


def _dot(a, b):
    return jnp.dot(a, b, preferred_element_type=F32)


def _rms(x, g):
    return x * lax.rsqrt(jnp.mean(x * x, axis=-1, keepdims=True) + EPS) * g


def _inproj_body(x_ref, g_ref, w_ref, cs_ref, sn_ref, qa_ref, ka_ref, va_ref, qr_ref, kr_ref,
                 vr_ref, gr_ref, ga_ref, gb_ref, *, offs, q_scale, k_scale, n_ret, dk):
    xn = _rms(x_ref[...], g_ref[...]).astype(BF16)

    def proj(i):
        return _dot(xn, w_ref[:, offs[i]:offs[i + 1]])

    qa_ref[...] = (proj(0) * q_scale).astype(BF16)
    ka_ref[...] = proj(1)
    va_ref[...] = proj(2)
    cs = cs_ref[...]
    sn = sn_ref[...]
    for ref, i, scale in ((qr_ref, 3, None), (kr_ref, 4, k_scale)):
        t = proj(i)
        for h in range(n_ret):
            th = t[:, h * dk:(h + 1) * dk]
            r = th * cs + pltpu.roll(th, dk // 2, axis=1) * sn
            ref[:, h * dk:(h + 1) * dk] = r if scale is None else r * scale
    vr_ref[...] = proj(5).astype(BF16)
    gr_ref[...] = proj(6)
    ga_ref[...] = proj(7)
    gb_ref[...] = proj(8)


def _sb_scores(z):
    t = jnp.log1p(jnp.exp(-jnp.abs(z)))
    return jnp.minimum(z, 0.0) - t, -jnp.maximum(z, 0.0) - t


def _split_bf16(x):
    hi = x.astype(BF16)
    return hi, (x - hi.astype(F32)).astype(BF16)


def _sb_prompt_body(q_ref, k_ref, v_ref, mk_ref, mv_ref, o_ref, kb, vb, *, tq, n_meta, dh):
    qi = pl.program_id(2)
    pad = tq - n_meta

    @pl.when(qi == 0)
    def _():
        kb[0:pad, :] = jnp.zeros((pad, LANES), BF16)
        vb[0:pad, :] = jnp.zeros((pad, LANES), BF16)
        kb[pad:tq, :] = mk_ref[...].astype(BF16)
        vb[pad:tq, :] = mv_ref[...].astype(BF16)
        kb[tq:, :] = k_ref[...].astype(BF16)
        vb[tq:, :] = v_ref[...].astype(BF16)

    q = q_ref[...]
    lane = lax.broadcasted_iota(jnp.int32, (tq, LANES), 1)
    row = lax.broadcasted_iota(jnp.int32, (tq, tq), 0)
    col = lax.broadcasted_iota(jnp.int32, (tq, tq), 1)
    suffix = jnp.where(row >= col, 1.0, 0.0).astype(BF16)
    diag_mask = col < row
    meta_mask = col >= pad

    for head in range(LANES // dh):
        qh = jnp.where(lane // dh == head, q, jnp.zeros_like(q))

        def block(koff, mask, carry, acc):
            kblk = kb[pl.ds(koff, tq), :]
            vblk = vb[pl.ds(koff, tq), :]
            z = lax.dot_general(qh, kblk, _NT, preferred_element_type=F32)
            log_beta, log_keep = _sb_scores(z)
            if mask is not None:
                log_keep = jnp.where(mask, log_keep, 0.0)
            hi, lo = _split_bf16(log_keep)
            incl = _dot(hi, suffix) + _dot(lo, suffix)
            w = jnp.exp(log_beta + (carry + incl - log_keep))
            if mask is not None:
                w = jnp.where(mask, w, 0.0)
            acc = acc + _dot(w.astype(BF16), vblk)
            return carry + incl[:, 0:1], acc

        carry = jnp.zeros((tq, 1), F32)
        acc = jnp.zeros((tq, LANES), F32)
        carry, acc = block(pl.multiple_of((qi + 1) * tq, tq), diag_mask, carry, acc)

        def left(it, ca):
            return block(pl.multiple_of((qi - it) * tq, tq), None, *ca)

        carry, acc = lax.fori_loop(0, qi, left, (carry, acc))
        carry, acc = block(0, meta_mask, carry, acc)
        o_ref[:, head * dh:(head + 1) * dh] = acc[:, head * dh:(head + 1) * dh].astype(o_ref.dtype)


def _sb_sample_body(q_ref, nk_ref, nv_ref, mk_ref, mv_ref, ck_ref, cv_ref, o_ref,
                    qbd, carry_ref, acc_ref, *, ts, n_meta, dh, sub):
    j = pl.program_id(1)
    width = q_ref.shape[1]
    n_heads = width // dh

    def head_mask(shape):
        r = lax.broadcasted_iota(jnp.int32, shape, 0)
        c = lax.broadcasted_iota(jnp.int32, shape, 1)
        return r // ts == c // dh

    def sweep(k, v, mask, carry, acc):
        n = k.shape[0]
        z = lax.dot_general(k.astype(BF16), qbd[...], _NT, preferred_element_type=F32)
        log_beta, log_keep = _sb_scores(z)
        if mask is not None:
            log_keep = jnp.where(mask, log_keep, 0.0)
        r = lax.broadcasted_iota(jnp.int32, (n, n), 0)
        c = lax.broadcasted_iota(jnp.int32, (n, n), 1)
        suffix = jnp.where(c >= r, 1.0, 0.0).astype(BF16)
        hi, lo = _split_bf16(log_keep)
        incl = _dot(suffix, hi) + _dot(suffix, lo)
        w = jnp.exp(log_beta + (carry + incl - log_keep))
        if mask is not None:
            w = jnp.where(mask, w, 0.0)
        acc = acc + lax.dot_general(w.astype(BF16), v.astype(BF16), _TN, preferred_element_type=F32)
        return carry + incl[0:1, :], acc

    @pl.when(j == 0)
    def _():
        qt = jnp.concatenate([q_ref[...]] * n_heads, axis=0)
        qbd[...] = jnp.where(head_mask((n_heads * ts, width)), qt, jnp.zeros_like(qt))
        key = lax.broadcasted_iota(jnp.int32, (ts, LANES), 0)
        qry = lax.broadcasted_iota(jnp.int32, (ts, LANES), 1) % ts
        carry, acc = sweep(nk_ref[...], nv_ref[...], key < qry,
                           jnp.zeros((1, LANES), F32), jnp.zeros((LANES, width), F32))
        carry_ref[...] = carry
        acc_ref[...] = acc

    carry = carry_ref[...]
    acc = acc_ref[...]
    chunk = ck_ref.shape[1]
    for s in range(chunk // sub - 1, -1, -1):
        carry, acc = sweep(ck_ref[0, s * sub:(s + 1) * sub, :], cv_ref[0, s * sub:(s + 1) * sub, :],
                           None, carry, acc)
    carry_ref[...] = carry
    acc_ref[...] = acc

    @pl.when(j == pl.num_programs(1) - 1)
    def _():
        _, acc = sweep(mk_ref[...], mv_ref[...], None, carry_ref[...], acc_ref[...])
        own = jnp.where(head_mask((n_heads * ts, width)), acc, 0.0)
        o_ref[...] = jnp.sum(own.reshape(n_heads, ts, width), axis=0).astype(o_ref.dtype)


def _ret_chunk(q, k, v, state, dec, rowdec, coldec, gc):
    qb = q.astype(BF16)
    inner = lax.dot_general(qb, k.astype(BF16), _NT, preferred_element_type=F32) * dec
    o = _dot(inner.astype(BF16), v) + _dot(qb, state.astype(BF16)) * rowdec
    kw = (k * coldec).astype(BF16)
    return gc * state + lax.dot_general(kw, v, _TN, preferred_element_type=F32), o


def _ret_gate(o, g):
    on = o * lax.rsqrt(jnp.mean(o * o, axis=-1, keepdims=True) + EPS)
    return (g * jax.nn.sigmoid(g)) * on


def _ret_prompt_body(q_ref, k_ref, v_ref, g_ref, mk_ref, mv_ref, dec_ref, rd_ref, cd_ref, gc_ref,
                     mcd_ref, y_ref, s_ref, *, chunk):
    kw = (mk_ref[...] * mcd_ref[0]).astype(BF16)
    state = lax.dot_general(kw, mv_ref[...], _TN, preferred_element_type=F32)
    dec, rd, cd, gc = dec_ref[0], rd_ref[0], cd_ref[0], gc_ref[0]

    def step(c, state):
        r = pl.ds(pl.multiple_of(c * chunk, chunk), chunk)
        state, o = _ret_chunk(q_ref[r, :], k_ref[r, :], v_ref[r, :], state, dec, rd, cd, gc)
        y_ref[r, :] = _ret_gate(o, g_ref[r, :]).astype(y_ref.dtype)
        return state

    s_ref[0, 0] = lax.fori_loop(0, q_ref.shape[0] // chunk, step, state)


def _ret_sample_body(q_ref, k_ref, v_ref, g_ref, s0_ref, dec_ref, rd_ref, cd_ref, gc_ref,
                     y_ref, s_ref):
    state, o = _ret_chunk(q_ref[...], k_ref[...], v_ref[...], s0_ref[0, 0], dec_ref[0], rd_ref[0],
                          cd_ref[0], gc_ref[0])
    y_ref[...] = _ret_gate(o, g_ref[...]).astype(y_ref.dtype)
    s_ref[0, 0] = state


def _pack_bf16_pairs(x):
    bits = lax.bitcast_convert_type(x.astype(BF16).astype(F32), jnp.uint32)
    w = x.shape[1] // 2
    return (bits[:, :w] >> 16) | bits[:, w:]


def _unpack_bf16_pairs(p):
    lo = lax.bitcast_convert_type(p << 16, F32).astype(BF16)
    hi = lax.bitcast_convert_type(p & jnp.uint32(0xFFFF0000), F32).astype(BF16)
    return lo, hi


def _merge_body(yap_ref, yrp_ref, yas_ref, yrs_ref, *rest, n_prompt, top_k):
    i = pl.program_id(0)

    @pl.when(i < n_prompt)
    def _():
        _merge_block(yap_ref, yrp_ref, *rest, top_k=top_k)

    @pl.when(i >= n_prompt)
    def _():
        _merge_block(yas_ref, yrs_ref, *rest, top_k=top_k)


def _merge_block(ya_ref, yr_ref, ga_ref, gb_ref, x_ref, wa_ref, wb_ref, wo_ref, nf_ref, wr_ref, br_ref,
                 h_ref, tok_ref, ids_ref, gt_ref, *, top_k):
    mix = (jax.nn.sigmoid(ga_ref[...]) * _dot(ya_ref[...], wa_ref[...])
           + jax.nn.sigmoid(gb_ref[...]) * _dot(yr_ref[...], wb_ref[...]))
    h = x_ref[...] + _dot(mix.astype(BF16), wo_ref[...])
    h_ref[...] = h
    tok = _rms(h, nf_ref[...])
    tok_ref[...] = _pack_bf16_pairs(tok)
    logits = lax.dot_general(wr_ref[...], tok.astype(BF16), _NT, preferred_element_type=F32) + br_ref[...]
    n_exp, tm = logits.shape
    eidx = lax.broadcasted_iota(jnp.int32, (n_exp, tm), 0).astype(F32)
    ids, tops = [], []
    for _ in range(top_k):
        m = jnp.max(logits, axis=0, keepdims=True)
        sel = jnp.min(jnp.where(logits == m, eidx, float(n_exp)), axis=0, keepdims=True)
        ids.append(sel)
        tops.append(m)
        logits = jnp.where(eidx == sel, -jnp.inf, logits)
    ex = [jnp.exp(t - tops[0]) for t in tops]
    inv = 1.0 / functools.reduce(lambda a, b: a + b, ex)
    ids_ref[...] = jnp.concatenate(ids, axis=0).astype(jnp.int32)
    gates = jnp.concatenate([e * inv for e in ex] + [jnp.zeros((LANES - top_k, tm), F32)], axis=0)
    gt_ref[...] = gates.T[:, :gt_ref.shape[1]]


def _rank_body(ids_ref, rank_ref, cnt_ref, run_ref, *, n_exp):
    i = pl.program_id(0)

    @pl.when(i == 0)
    def _():
        run_ref[...] = jnp.zeros_like(run_ref)

    top_k, tb = ids_ref.shape
    ids = ids_ref[...].astype(F32)
    eidx = lax.broadcasted_iota(jnp.int32, (n_exp, tb), 0).astype(F32)
    r = lax.broadcasted_iota(jnp.int32, (tb, tb), 0)
    c = lax.broadcasted_iota(jnp.int32, (tb, tb), 1)
    before = jnp.where(r < c, 1.0, 0.0).astype(BF16)
    run = run_ref[...]
    ranks = []
    for k in range(top_k):
        onehot = jnp.where(ids[k:k + 1, :] == eidx, 1.0, 0.0)
        pos = _dot(onehot.astype(BF16), before) + run
        ranks.append(jnp.sum(onehot * pos, axis=0, keepdims=True))
        run = run + jnp.sum(onehot, axis=1, keepdims=True)
    rank_ref[...] = jnp.concatenate(ranks, axis=0).astype(jnp.int32)
    run_ref[...] = run
    cnt_ref[...] = run


def _row_copy(src, dst, sem):
    return pltpu.make_async_copy(src, dst, sem)


def _dispatch_body(dest_ref, tok_ref, xs_out, zeros, sem, *, top_k):
    tb = tok_ref.shape[0]
    zb = zeros.shape[0]

    @pl.when(pl.program_id(0) == 0)
    def _():
        zeros[...] = jnp.zeros_like(zeros)

        def fill(b, _):
            _row_copy(zeros, xs_out.at[pl.ds(pl.multiple_of(b * zb, zb), zb), :], sem).start()
            return 0

        def done(b, _):
            _row_copy(zeros, xs_out.at[pl.ds(pl.multiple_of(b * zb, zb), zb), :], sem).wait()
            return 0

        lax.fori_loop(0, xs_out.shape[0] // zb, fill, 0)
        lax.fori_loop(0, xs_out.shape[0] // zb, done, 0)

    def issue(t, _):
        for k in range(top_k):
            _row_copy(tok_ref.at[pl.ds(t, 1), :], xs_out.at[pl.ds(dest_ref[k, t], 1), :], sem).start()
        return 0

    def drain(t, _):
        for k in range(top_k):
            _row_copy(tok_ref.at[pl.ds(t, 1), :], xs_out.at[pl.ds(dest_ref[k, t], 1), :], sem).wait()
        return 0

    lax.fori_loop(0, tb, issue, 0)
    lax.fori_loop(0, tb, drain, 0)


def _expert_body(be_ref, nu_ref, x_ref, wgu_ref, bgu_ref, wd_ref, bd_ref, o_ref, *, d_ff):
    del be_ref
    used = pl.program_id(0) < nu_ref[0]

    @pl.when(used)
    def _():
        lo, hi = _unpack_bf16_pairs(x_ref[...])
        half = lo.shape[1]
        h = _dot(lo, wgu_ref[0, :half, :]) + _dot(hi, wgu_ref[0, half:, :]) + bgu_ref[0]
        g = jnp.minimum(h[:, :d_ff], SWIGLU_LIMIT)
        u = jnp.clip(h[:, d_ff:], -SWIGLU_LIMIT, SWIGLU_LIMIT)
        act = (u + 1.0) * (g * jax.nn.sigmoid(SWIGLU_ALPHA * g))
        o_ref[...] = _dot(act.astype(BF16), wd_ref[0]) + bd_ref[0]

    @pl.when(jnp.logical_not(used))
    def _():
        o_ref[...] = jnp.zeros_like(o_ref)


def _combine_body(dest_ref, h_ref, gt_ref, nf_ref, os_hbm, y_ref, buf, sem, *, top_k):
    tb = h_ref.shape[0]

    def issue(t, _):
        for k in range(top_k):
            _row_copy(os_hbm.at[pl.ds(dest_ref[k, t], 1), :], buf.at[k, pl.ds(t, 1), :], sem).start()
        return 0

    def drain(t, _):
        for k in range(top_k):
            _row_copy(os_hbm.at[pl.ds(dest_ref[k, t], 1), :], buf.at[k, pl.ds(t, 1), :], sem).wait()
        return 0

    lax.fori_loop(0, tb, issue, 0)
    lax.fori_loop(0, tb, drain, 0)
    gt = gt_ref[...]
    f = buf[0] * gt[:, 0:1]
    for k in range(1, top_k):
        f = f + buf[k] * gt[:, k:k + 1]
    y_ref[...] = _rms(h_ref[...] + f, nf_ref[...])


def _largest_pow2_divisor(n, cap):
    t = cap
    while n % t:
        t //= 2
    return t


def _rotation_tables(pos, half):
    inv_freq = 1.0 / (ROPE_BASE ** jnp.linspace(0.0, 1.0, half, dtype=F32))
    ang = pos.astype(F32)[:, None] * inv_freq[None, :]
    cos, sin = jnp.cos(ang), jnp.sin(ang)
    return jnp.concatenate([cos, cos], axis=1), jnp.concatenate([-sin, sin], axis=1)


def _decay_tables(log_g, c):
    idx = jnp.arange(c, dtype=F32)
    lg = log_g[:, None]
    rel = idx[:, None] - idx[None, :]
    dec = jnp.where(rel >= 0, jnp.exp(lg[:, :, None] * jnp.maximum(rel, 0.0)), 0.0)
    rowdec = jnp.exp(lg * (idx + 1.0))[:, :, None]
    coldec = jnp.exp(lg * (c - 1.0 - idx))[:, :, None]
    gc = jnp.exp(lg * c)[:, :, None]
    return dec, rowdec, coldec, gc


def kernel(x_prompt, x_sample, cache_sb_k, cache_sb_v, state_ret, meta_tokens, norm_mix, w_in, w_branch_sb, w_branch_ret, w_out, norm_ffn, w_router, b_router, w_gate_up, b_gate_up, w_down, b_down, norm_final):
    B, SEQ, D = x_prompt.shape
    Bs, Ts, _ = x_sample.shape
    depth, _, past, h_sb, dh = cache_sb_k.shape
    _, _, h_ret, dk, dv = state_ret.shape
    n_meta = meta_tokens.shape[0]
    n_exp = w_router.shape[-1]
    d_ff = w_down.shape[-2]
    sbw, rqk, rv = h_sb * dh, h_ret * dk, h_ret * dv
    assert depth == 1, "one layer: the meta-token rows feed nothing after the mixers"
    assert dk == LANES and 2 * dh == LANES and h_sb * Ts == LANES and n_meta % 16 == 0

    TQ = 128
    assert SEQ % TQ == 0 and n_meta <= TQ
    t_p, t_s = B * SEQ, Bs * Ts
    t_moe = t_p + t_s
    TM = _largest_pow2_divisor(t_moe, 512)
    assert TM >= 16 and SEQ % TM == 0 and t_p % TM == 0 and t_moe % n_meta == 0 and t_p % Ts == 0
    t_all = -(-(t_moe + n_meta) // TM) * TM
    n_p, n_moe, n_all = t_p // TM, t_moe // TM, t_all // TM
    meta_blk = t_moe // n_meta

    x_all = jnp.concatenate([x_prompt.reshape(t_p, D), x_sample.reshape(t_s, D), meta_tokens.astype(F32),
                             jnp.zeros((t_all - t_moe - n_meta, D), F32)], axis=0)

    sizes = (sbw,) * 3 + (rqk,) * 2 + (rv,) * 2 + (D,) * 2
    offs = tuple(sum(sizes[:i]) for i in range(len(sizes) + 1))
    small_pos = jnp.concatenate([jnp.tile(past + jnp.arange(Ts), Bs), jnp.arange(n_meta) - n_meta,
                                 jnp.zeros((t_all - t_moe - n_meta,), jnp.int32)])
    cs_tab, sn_tab = _rotation_tables(jnp.concatenate([jnp.arange(SEQ), small_pos]), dk // 2)
    n_sq = SEQ // TM

    def tab_map(i):
        return (jnp.where(i < n_p, i % n_sq, n_sq + i - n_p), 0)

    row = lambda w: pl.BlockSpec((TM, w), lambda i: (i, 0))
    out_dt = (BF16, F32, F32, F32, F32, BF16, F32, F32, F32)
    qa, ka, va, qr, kr, vr, gr, ga, gb = pl.pallas_call(
        functools.partial(_inproj_body, offs=offs, q_scale=dh ** -0.5, k_scale=dk ** -0.5, n_ret=h_ret, dk=dk),
        grid=(n_all,),
        in_specs=[row(D), pl.BlockSpec((1, D), lambda i: (0, 0)),
                  pl.BlockSpec((D, offs[-1]), lambda i: (0, 0), pipeline_mode=pl.Buffered(1)),
                  pl.BlockSpec((TM, dk), tab_map), pl.BlockSpec((TM, dk), tab_map)],
        out_specs=[row(w) for w in sizes],
        out_shape=[jax.ShapeDtypeStruct((t_all, w), dt) for w, dt in zip(sizes, out_dt)],
        compiler_params=_params("parallel"), name="inproj",
    )(x_all, norm_mix[0][None, :], w_in[0].astype(BF16), cs_tab, sn_tab)

    n_q = SEQ // TQ
    n_pair = sbw // LANES
    ya_p = pl.pallas_call(
        functools.partial(_sb_prompt_body, tq=TQ, n_meta=n_meta, dh=dh),
        grid=(B, n_pair, n_q),
        in_specs=[pl.BlockSpec((TQ, LANES), lambda b, p, q: (b * n_q + q, p)),
                  pl.BlockSpec((SEQ, LANES), lambda b, p, q: (b, p)),
                  pl.BlockSpec((SEQ, LANES), lambda b, p, q: (b, p)),
                  pl.BlockSpec((n_meta, LANES), lambda b, p, q: (meta_blk, p)),
                  pl.BlockSpec((n_meta, LANES), lambda b, p, q: (meta_blk, p))],
        out_specs=pl.BlockSpec((TQ, LANES), lambda b, p, q: (b * n_q + q, p)),
        out_shape=jax.ShapeDtypeStruct((t_p, sbw), BF16),
        scratch_shapes=[pltpu.VMEM((TQ + SEQ, LANES), BF16), pltpu.VMEM((TQ + SEQ, LANES), BF16)],
        compiler_params=_params("parallel", "parallel", "arbitrary"), name="sb_prompt",
    )(qa, ka, va, ka, va)

    KC = _largest_pow2_divisor(past, 512)
    SUB = min(KC, 256)
    n_kc = past // KC
    smp = lambda w: pl.BlockSpec((Ts, w), lambda b, j: (t_p // Ts + b, 0))
    met = lambda w: pl.BlockSpec((n_meta, w), lambda b, j: (meta_blk, 0))
    cache = pl.BlockSpec((1, KC, sbw), lambda b, j: (b, n_kc - 1 - j, 0))
    ya_s = pl.pallas_call(
        functools.partial(_sb_sample_body, ts=Ts, n_meta=n_meta, dh=dh, sub=SUB),
        grid=(Bs, n_kc),
        in_specs=[smp(sbw), smp(sbw), smp(sbw), met(sbw), met(sbw), cache, cache],
        out_specs=pl.BlockSpec((Ts, sbw), lambda b, j: (b, 0)),
        out_shape=jax.ShapeDtypeStruct((t_s, sbw), BF16),
        scratch_shapes=[pltpu.VMEM((LANES, sbw), BF16), pltpu.VMEM((1, LANES), F32), pltpu.VMEM((LANES, sbw), F32)],
        compiler_params=_params("parallel", "arbitrary"), name="sb_sample",
    )(qa, ka, va, ka, va, cache_sb_k[0].reshape(Bs, past, sbw), cache_sb_v[0].reshape(Bs, past, sbw))

    log_g = jnp.log(1.0 - 2.0 ** (-5.0 - jnp.arange(h_ret, dtype=F32)))
    RC = _largest_pow2_divisor(SEQ, 256)
    dec, rd, cd, gc = _decay_tables(log_g, RC)
    _, _, mcd, _ = _decay_tables(log_g, n_meta)
    per_b = lambda w: pl.BlockSpec((SEQ, w), lambda b, h: (b, h))
    met_h = lambda w: pl.BlockSpec((n_meta, w), lambda b, h: (meta_blk, h))
    tab = lambda a: pl.BlockSpec((1,) + a.shape[1:], lambda b, h: (h, 0, 0))
    state_spec = pl.BlockSpec((1, 1, dk, dv), lambda b, h: (b, h, 0, 0))
    yr_p, s_p = pl.pallas_call(
        functools.partial(_ret_prompt_body, chunk=RC),
        grid=(B, h_ret),
        in_specs=[per_b(dk), per_b(dk), per_b(dv), per_b(dv), met_h(dk), met_h(dv),
                  tab(dec), tab(rd), tab(cd), tab(gc), tab(mcd)],
        out_specs=[per_b(dv), state_spec],
        out_shape=[jax.ShapeDtypeStruct((t_p, rv), BF16), jax.ShapeDtypeStruct((B, h_ret, dk, dv), F32)],
        compiler_params=_params("parallel", "parallel"), name="ret_prompt",
    )(qr, kr, vr, gr, kr, vr, dec, rd, cd, gc, mcd)

    dec_s, rd_s, cd_s, gc_s = _decay_tables(log_g, Ts)
    smp_h = lambda w: pl.BlockSpec((Ts, w), lambda b, h: (t_p // Ts + b, h))
    yr_s, s_s = pl.pallas_call(
        _ret_sample_body,
        grid=(Bs, h_ret),
        in_specs=[smp_h(dk), smp_h(dk), smp_h(dv), smp_h(dv), state_spec,
                  tab(dec_s), tab(rd_s), tab(cd_s), tab(gc_s)],
        out_specs=[pl.BlockSpec((Ts, dv), lambda b, h: (b, h)), state_spec],
        out_shape=[jax.ShapeDtypeStruct((t_s, rv), BF16), jax.ShapeDtypeStruct((Bs, h_ret, dk, dv), F32)],
        compiler_params=_params("parallel", "parallel"), name="ret_sample",
    )(qr, kr, vr, gr, state_ret[0], dec_s, rd_s, cd_s, gc_s)

    GW = 8
    const = lambda a: pl.BlockSpec(a.shape, lambda i: (0,) * a.ndim)
    prm = lambda w: pl.BlockSpec((TM, w), lambda i: (jnp.minimum(i, n_p - 1), 0))
    smp_m = lambda w: pl.BlockSpec((TM, w), lambda i: (jnp.maximum(i - n_p, 0), 0))
    wa, wb, wo = w_branch_sb[0].astype(BF16), w_branch_ret[0].astype(BF16), w_out[0].astype(BF16)
    nf, wr_t, br = norm_ffn[0][None, :], w_router[0].T.astype(BF16), b_router[0][:, None]
    h1, tok, ids, gates_t = pl.pallas_call(
        functools.partial(_merge_body, n_prompt=n_p, top_k=TOP_K),
        grid=(n_moe,),
        in_specs=[prm(sbw), prm(rv), smp_m(sbw), smp_m(rv), row(D), row(D), row(D), const(wa), const(wb),
                  const(wo), const(nf), const(wr_t), const(br)],
        out_specs=[row(D), row(D // 2), pl.BlockSpec((TOP_K, TM), lambda i: (0, i)), row(GW)],
        out_shape=[jax.ShapeDtypeStruct((t_moe, D), F32), jax.ShapeDtypeStruct((t_moe, D // 2), jnp.uint32),
                   jax.ShapeDtypeStruct((TOP_K, t_moe), jnp.int32), jax.ShapeDtypeStruct((t_moe, GW), F32)],
        compiler_params=_params("parallel"), name="merge_router",
    )(ya_p, yr_p, ya_s, yr_s, ga, gb, x_all, wa, wb, wo, nf, wr_t, br)

    rank, cnt = pl.pallas_call(
        functools.partial(_rank_body, n_exp=n_exp),
        grid=(n_moe,),
        in_specs=[pl.BlockSpec((TOP_K, TM), lambda i: (0, i))],
        out_specs=[pl.BlockSpec((TOP_K, TM), lambda i: (0, i)), pl.BlockSpec((n_exp, 1), lambda i: (0, 0))],
        out_shape=[jax.ShapeDtypeStruct((TOP_K, t_moe), jnp.int32), jax.ShapeDtypeStruct((n_exp, 1), F32)],
        scratch_shapes=[pltpu.VMEM((n_exp, 1), F32)],
        compiler_params=_params("arbitrary"), name="moe_rank",
    )(ids)
    BM = TM
    counts = cnt[:, 0].astype(jnp.int32)
    padded = (counts + BM - 1) // BM * BM
    pad_end = jnp.cumsum(padded)
    dest = (pad_end - padded)[ids] + rank
    n_blk = -(-(t_moe * TOP_K) // BM) + n_exp
    block_e = jnp.minimum(jnp.searchsorted(pad_end, jnp.arange(n_blk) * BM, side="right"), n_exp - 1).astype(jnp.int32)
    n_used = (pad_end[-1:] // BM).astype(jnp.int32)

    TB = min(TM, 256)
    dest_spec = lambda off: pl.BlockSpec((TOP_K, TB), lambda i: (0, i + off), memory_space=pltpu.SMEM)
    xs = pl.pallas_call(
        functools.partial(_dispatch_body, top_k=TOP_K),
        grid=(t_moe // TB,),
        in_specs=[dest_spec(0), pl.BlockSpec((TB, D // 2), lambda i: (i, 0))],
        out_specs=pl.BlockSpec(memory_space=pl.ANY),
        out_shape=jax.ShapeDtypeStruct((n_blk * BM, D // 2), jnp.uint32),
        scratch_shapes=[pltpu.VMEM((BM, D // 2), jnp.uint32), pltpu.SemaphoreType.DMA],
        compiler_params=_params("arbitrary"), name="moe_dispatch",
    )(dest, tok)

    blk = lambda i, be, nu: jnp.minimum(i, nu[0] - 1)
    exp_w = lambda a: pl.BlockSpec((1,) + a.shape[1:], lambda i, be, nu: (be[blk(i, be, nu)],) + (0,) * (a.ndim - 1))
    wgu, bgu = w_gate_up[0].astype(BF16), b_gate_up[0][:, None, :]
    wd, bd = w_down[0].astype(BF16), b_down[0][:, None, :]
    os_ = pl.pallas_call(
        functools.partial(_expert_body, d_ff=d_ff),
        grid_spec=pltpu.PrefetchScalarGridSpec(
            num_scalar_prefetch=2, grid=(n_blk,),
            in_specs=[pl.BlockSpec((BM, D // 2), lambda i, be, nu: (blk(i, be, nu), 0)),
                      exp_w(wgu), exp_w(bgu), exp_w(wd), exp_w(bd)],
            out_specs=pl.BlockSpec((BM, D), lambda i, be, nu: (i, 0))),
        out_shape=jax.ShapeDtypeStruct((n_blk * BM, D), F32),
        compiler_params=_params("arbitrary"), name="moe_experts",
    )(block_e, n_used, xs, wgu, bgu, wd, bd)

    def combine(off, n):
        return pl.pallas_call(
            functools.partial(_combine_body, top_k=TOP_K),
            grid=(n,),
            in_specs=[dest_spec(off), pl.BlockSpec((TB, D), lambda i: (i + off, 0)),
                      pl.BlockSpec((TB, GW), lambda i: (i + off, 0)), pl.BlockSpec((1, D), lambda i: (0, 0)),
                      pl.BlockSpec(memory_space=pl.ANY)],
            out_specs=pl.BlockSpec((TB, D), lambda i: (i, 0)),
            out_shape=jax.ShapeDtypeStruct((n * TB, D), F32),
            scratch_shapes=[pltpu.VMEM((TOP_K, TB, D), F32), pltpu.SemaphoreType.DMA],
            compiler_params=_params("arbitrary"), name="moe_combine",
        )(dest, h1, gates_t, norm_final[None, :], os_)

    y_prompt = combine(0, t_p // TB).reshape(B, SEQ, D)
    y_sample = combine(t_p // TB, t_s // TB).reshape(Bs, Ts, D)

    def with_meta(a):
        meta_rows = jnp.broadcast_to(a[t_moe:t_moe + n_meta][None], (B, n_meta, sbw))
        return jnp.concatenate([meta_rows, a[:t_p].reshape(B, SEQ, sbw)], axis=1).reshape(1, B, n_meta + SEQ, h_sb, dh)

    return (y_prompt, y_sample, with_meta(ka), with_meta(va), s_p[None],
            ka[t_p:t_moe].reshape(1, Bs, Ts, h_sb, dh), va[t_p:t_moe].reshape(1, Bs, Ts, h_sb, dh), s_s[None])
```

```python
import functools

import jax
import jax.numpy as jnp
from jax import lax
from jax.experimental import pallas as pl
from jax.experimental.pallas import tpu as pltpu

F32 = jnp.float32
BF16 = jnp.bfloat16

EPS = 1e-6
ROPE_BASE = 10000.0
TOP_K = 4
SWIGLU_LIMIT = 7.0
SWIGLU_ALPHA = 1.702

LANES = 128
VMEM_LIMIT = 56 << 20

_NT = (((1,), (1,)), ((), ()))
_TN = (((0,), (0,)), ((), ()))


def _params(*sem):
    return pltpu.CompilerParams(dimension_semantics=sem, vmem_limit_bytes=VMEM_LIMIT)


def _dot(a, b):
    return jnp.dot(a, b, preferred_element_type=F32)


def _rms(x, g):
    return x * lax.rsqrt(jnp.mean(x * x, axis=-1, keepdims=True) + EPS) * g


def _inproj_body(x_ref, g_ref, w_ref, cs_ref, sn_ref, qa_ref, ka_ref, va_ref, qr_ref, kr_ref,
                 vr_ref, gr_ref, ga_ref, gb_ref, *, offs, q_scale, k_scale, n_ret, dk):
    xn = _rms(x_ref[...], g_ref[...]).astype(BF16)

    def proj(i):
        return _dot(xn, w_ref[:, offs[i]:offs[i + 1]])

    qa_ref[...] = (proj(0) * q_scale).astype(BF16)
    ka_ref[...] = proj(1)
    va_ref[...] = proj(2)
    cs = cs_ref[...]
    sn = sn_ref[...]
    for ref, i, scale in ((qr_ref, 3, None), (kr_ref, 4, k_scale)):
        t = proj(i)
        for h in range(n_ret):
            th = t[:, h * dk:(h + 1) * dk]
            r = th * cs + pltpu.roll(th, dk // 2, axis=1) * sn
            ref[:, h * dk:(h + 1) * dk] = r if scale is None else r * scale
    vr_ref[...] = proj(5).astype(BF16)
    gr_ref[...] = proj(6)
    ga_ref[...] = proj(7)
    gb_ref[...] = proj(8)


def _sb_scores(z):
    log_beta = jnp.minimum(z, 0.0) - jnp.log(1.0 + jnp.exp(-jnp.abs(z)))
    return log_beta, log_beta - z


def _split_bf16(x):
    hi = x.astype(BF16)
    return hi, (x - hi.astype(F32)).astype(BF16)


def _suffix_ones(n):
    r = lax.broadcasted_iota(jnp.int32, (2 * n, n), 0) & (n - 1)
    c = lax.broadcasted_iota(jnp.int32, (2 * n, n), 1)
    return jnp.where(r >= c, 1.0, 0.0).astype(BF16)


def _sb_block(qs, kblk, vblk, suffix2, mask, carry, acc):
    z = lax.dot_general(qs, kblk, _NT, preferred_element_type=F32)
    _, log_keep = _sb_scores(z)
    if mask is not None:
        log_keep = jnp.where(mask, log_keep, 0.0)
    hi, lo = _split_bf16(log_keep)
    incl = _dot(jnp.concatenate([hi, lo], axis=1), suffix2)
    w = jnp.exp(z + (incl + carry))
    if mask is not None:
        w = jnp.where(mask, w, 0.0)
    return carry + incl[:, 0:1], acc + _dot(w.astype(BF16), vblk)


def _sb_prompt_body(q_ref, k_ref, v_ref, mk_ref, mv_ref, o_ref, kb, vb, carry_ref, acc_ref, *, tq, n_meta, dh):
    qi = pl.program_id(2)
    tk = tq // 2
    mpad = LANES - n_meta
    n_heads = LANES // dh

    @pl.when(qi == 0)
    def _():
        kb[0:mpad, :] = jnp.zeros((mpad, LANES), BF16)
        vb[0:mpad, :] = jnp.zeros((mpad, LANES), BF16)
        kb[mpad:LANES, :] = mk_ref[...].astype(BF16)
        vb[mpad:LANES, :] = mv_ref[...].astype(BF16)
        kb[LANES:, :] = k_ref[...].astype(BF16)
        vb[LANES:, :] = v_ref[...].astype(BF16)

    q = q_ref[...]
    lane = lax.broadcasted_iota(jnp.int32, (tk, LANES), 1)
    parts = []
    for half in range(2):
        qh = q[half * tk:(half + 1) * tk]
        for head in range(n_heads):
            parts.append(jnp.where(lane // dh == head, qh, jnp.zeros_like(qh)))
    qs = jnp.concatenate(parts, axis=0)
    m_half = n_heads * tk
    suffix2 = _suffix_ones(tk)
    base = pl.multiple_of(LANES + qi * tq, LANES)

    r = lax.broadcasted_iota(jnp.int32, (m_half, tk), 0) & (tk - 1)
    c = lax.broadcasted_iota(jnp.int32, (m_half, tk), 1)
    carry_b, acc_b = _sb_block(qs[m_half:], kb[pl.ds(base + tk, tk), :], vb[pl.ds(base + tk, tk), :], suffix2,
                               c < r, jnp.zeros((m_half, 1), F32), jnp.zeros((m_half, LANES), F32))
    carry = jnp.concatenate([jnp.zeros((m_half, 1), F32), carry_b], axis=0)
    acc = jnp.concatenate([jnp.zeros((m_half, LANES), F32), acc_b], axis=0)
    r = lax.broadcasted_iota(jnp.int32, (2 * m_half, tk), 0)
    c = lax.broadcasted_iota(jnp.int32, (2 * m_half, tk), 1)
    carry, acc = _sb_block(qs, kb[pl.ds(base, tk), :], vb[pl.ds(base, tk), :], suffix2,
                           (r >= m_half) | (c < (r & (tk - 1))), carry, acc)
    carry_ref[...] = carry
    acc_ref[...] = acc

    def left(it, _):
        off = pl.multiple_of(base - (it + 1) * tk, LANES)
        carry, acc = _sb_block(qs, kb[pl.ds(off, tk), :], vb[pl.ds(off, tk), :], suffix2, None,
                               carry_ref[...], acc_ref[...])
        carry_ref[...] = carry
        acc_ref[...] = acc
        return 0

    lax.fori_loop(0, 2 * qi, left, 0)
    c = lax.broadcasted_iota(jnp.int32, (2 * m_half, LANES), 1)
    _, acc = _sb_block(qs, kb[0:LANES, :], vb[0:LANES, :], _suffix_ones(LANES), c >= mpad,
                       carry_ref[...], acc_ref[...])
    for half in range(2):
        for head in range(n_heads):
            rows = (half * n_heads + head) * tk
            o_ref[half * tk:(half + 1) * tk, head * dh:(head + 1) * dh] = (
                acc[rows:rows + tk, head * dh:(head + 1) * dh].astype(o_ref.dtype))


def _sb_sample_body(q_ref, nk_ref, nv_ref, mk_ref, mv_ref, ck_ref, cv_ref, o_ref,
                    qbd, carry_ref, acc_ref, *, ts, n_meta, dh, sub):
    j = pl.program_id(1)
    width = q_ref.shape[1]
    n_heads = width // dh

    def head_mask(shape):
        r = lax.broadcasted_iota(jnp.int32, shape, 0)
        c = lax.broadcasted_iota(jnp.int32, shape, 1)
        return r // ts == c // dh

    def sweep(k, v, mask, carry, acc):
        n = k.shape[0]
        z = lax.dot_general(k.astype(BF16), qbd[...], _NT, preferred_element_type=F32)
        log_beta, log_keep = _sb_scores(z)
        if mask is not None:
            log_keep = jnp.where(mask, log_keep, 0.0)
        r = lax.broadcasted_iota(jnp.int32, (n, n), 0)
        c = lax.broadcasted_iota(jnp.int32, (n, n), 1)
        suffix = jnp.where(c >= r, 1.0, 0.0).astype(BF16)
        hi, lo = _split_bf16(log_keep)
        incl = _dot(suffix, hi) + _dot(suffix, lo)
        w = jnp.exp(log_beta + (carry + incl - log_keep))
        if mask is not None:
            w = jnp.where(mask, w, 0.0)
        acc = acc + lax.dot_general(w.astype(BF16), v.astype(BF16), _TN, preferred_element_type=F32)
        return carry + incl[0:1, :], acc

    @pl.when(j == 0)
    def _():
        qt = jnp.concatenate([q_ref[...]] * n_heads, axis=0)
        qbd[...] = jnp.where(head_mask((n_heads * ts, width)), qt, jnp.zeros_like(qt))
        key = lax.broadcasted_iota(jnp.int32, (ts, LANES), 0)
        qry = lax.broadcasted_iota(jnp.int32, (ts, LANES), 1) % ts
        carry, acc = sweep(nk_ref[...], nv_ref[...], key < qry,
                           jnp.zeros((1, LANES), F32), jnp.zeros((LANES, width), F32))
        carry_ref[...] = carry
        acc_ref[...] = acc

    carry = carry_ref[...]
    acc = acc_ref[...]
    chunk = ck_ref.shape[1]
    for s in range(chunk // sub - 1, -1, -1):
        carry, acc = sweep(ck_ref[0, s * sub:(s + 1) * sub, :], cv_ref[0, s * sub:(s + 1) * sub, :],
                           None, carry, acc)
    carry_ref[...] = carry
    acc_ref[...] = acc

    @pl.when(j == pl.num_programs(1) - 1)
    def _():
        _, acc = sweep(mk_ref[...], mv_ref[...], None, carry_ref[...], acc_ref[...])
        own = jnp.where(head_mask((n_heads * ts, width)), acc, 0.0)
        o_ref[...] = jnp.sum(own.reshape(n_heads, ts, width), axis=0).astype(o_ref.dtype)


def _ret_chunk(q, k, v, state, dec, rowdec, coldec, gc):
    qb = q.astype(BF16)
    inner = lax.dot_general(qb, k.astype(BF16), _NT, preferred_element_type=F32) * dec
    o = _dot(inner.astype(BF16), v) + _dot(qb, state.astype(BF16)) * rowdec
    kw = (k * coldec).astype(BF16)
    return gc * state + lax.dot_general(kw, v, _TN, preferred_element_type=F32), o


def _ret_gate(o, g):
    on = o * lax.rsqrt(jnp.mean(o * o, axis=-1, keepdims=True) + EPS)
    return (g * jax.nn.sigmoid(g)) * on


def _ret_prompt_body(q_ref, k_ref, v_ref, g_ref, mk_ref, mv_ref, dec_ref, rd_ref, cd_ref, gc_ref,
                     mcd_ref, y_ref, s_ref, *, chunk):
    kw = (mk_ref[...] * mcd_ref[0]).astype(BF16)
    state = lax.dot_general(kw, mv_ref[...], _TN, preferred_element_type=F32)
    dec, rd, cd, gc = dec_ref[0], rd_ref[0], cd_ref[0], gc_ref[0]

    def step(c, state):
        r = pl.ds(pl.multiple_of(c * chunk, chunk), chunk)
        state, o = _ret_chunk(q_ref[r, :], k_ref[r, :], v_ref[r, :], state, dec, rd, cd, gc)
        y_ref[r, :] = _ret_gate(o, g_ref[r, :]).astype(y_ref.dtype)
        return state

    s_ref[0, 0] = lax.fori_loop(0, q_ref.shape[0] // chunk, step, state)


def _ret_sample_body(q_ref, k_ref, v_ref, g_ref, s0_ref, dec_ref, rd_ref, cd_ref, gc_ref,
                     y_ref, s_ref):
    state, o = _ret_chunk(q_ref[...], k_ref[...], v_ref[...], s0_ref[0, 0], dec_ref[0], rd_ref[0],
                          cd_ref[0], gc_ref[0])
    y_ref[...] = _ret_gate(o, g_ref[...]).astype(y_ref.dtype)
    s_ref[0, 0] = state


def _pack_bf16_pairs(x):
    bits = lax.bitcast_convert_type(x.astype(BF16).astype(F32), jnp.uint32)
    w = x.shape[1] // 2
    return (bits[:, :w] >> 16) | bits[:, w:]


def _unpack_bf16_pairs(p):
    lo = lax.bitcast_convert_type(p << 16, F32).astype(BF16)
    hi = lax.bitcast_convert_type(p & jnp.uint32(0xFFFF0000), F32).astype(BF16)
    return lo, hi


def _merge_body(yap_ref, yrp_ref, yas_ref, yrs_ref, *rest, n_prompt, top_k):
    i = pl.program_id(0)

    @pl.when(i < n_prompt)
    def _():
        _merge_block(yap_ref, yrp_ref, *rest, top_k=top_k)

    @pl.when(i >= n_prompt)
    def _():
        _merge_block(yas_ref, yrs_ref, *rest, top_k=top_k)


def _merge_block(ya_ref, yr_ref, ga_ref, gb_ref, x_ref, wa_ref, wb_ref, wo_ref, nf_ref, wr_ref, br_ref,
                 h_ref, tok_ref, ids_ref, gt_ref, *, top_k):
    mix = (jax.nn.sigmoid(ga_ref[...]) * _dot(ya_ref[...], wa_ref[...])
           + jax.nn.sigmoid(gb_ref[...]) * _dot(yr_ref[...], wb_ref[...]))
    h = x_ref[...] + _dot(mix.astype(BF16), wo_ref[...])
    h_ref[...] = h
    tok = _rms(h, nf_ref[...])
    tok_ref[...] = _pack_bf16_pairs(tok)
    logits = lax.dot_general(wr_ref[...], tok.astype(BF16), _NT, preferred_element_type=F32) + br_ref[...]
    n_exp, tm = logits.shape
    eidx = lax.broadcasted_iota(jnp.int32, (n_exp, tm), 0).astype(F32)
    ids, tops = [], []
    for _ in range(top_k):
        m = jnp.max(logits, axis=0, keepdims=True)
        sel = jnp.min(jnp.where(logits == m, eidx, float(n_exp)), axis=0, keepdims=True)
        ids.append(sel)
        tops.append(m)
        logits = jnp.where(eidx == sel, -jnp.inf, logits)
    ex = [jnp.exp(t - tops[0]) for t in tops]
    inv = 1.0 / functools.reduce(lambda a, b: a + b, ex)
    ids_ref[...] = jnp.concatenate(ids, axis=0).astype(jnp.int32)
    gates = jnp.concatenate([e * inv for e in ex] + [jnp.zeros((LANES - top_k, tm), F32)], axis=0)
    gt_ref[...] = gates.T[:, :gt_ref.shape[1]]


def _rank_body(ids_ref, rank_ref, cnt_ref, run_ref, *, n_exp):
    i = pl.program_id(0)

    @pl.when(i == 0)
    def _():
        run_ref[...] = jnp.zeros_like(run_ref)

    top_k, tb = ids_ref.shape
    ids = ids_ref[...].astype(F32)
    eidx = lax.broadcasted_iota(jnp.int32, (n_exp, tb), 0).astype(F32)
    r = lax.broadcasted_iota(jnp.int32, (tb, tb), 0)
    c = lax.broadcasted_iota(jnp.int32, (tb, tb), 1)
    before = jnp.where(r < c, 1.0, 0.0).astype(BF16)
    run = run_ref[...]
    ranks = []
    for k in range(top_k):
        onehot = jnp.where(ids[k:k + 1, :] == eidx, 1.0, 0.0)
        pos = _dot(onehot.astype(BF16), before) + run
        ranks.append(jnp.sum(onehot * pos, axis=0, keepdims=True))
        run = run + jnp.sum(onehot, axis=1, keepdims=True)
    rank_ref[...] = jnp.concatenate(ranks, axis=0).astype(jnp.int32)
    run_ref[...] = run
    cnt_ref[...] = run


def _row_copy(src, dst, sem):
    return pltpu.make_async_copy(src, dst, sem)


def _dispatch_body(dest_ref, tok_ref, xs_out, zeros, sem, *, top_k):
    tb = tok_ref.shape[0]
    zb = zeros.shape[0]

    @pl.when(pl.program_id(0) == 0)
    def _():
        zeros[...] = jnp.zeros_like(zeros)

        def fill(b, _):
            _row_copy(zeros, xs_out.at[pl.ds(pl.multiple_of(b * zb, zb), zb), :], sem).start()
            return 0

        def done(b, _):
            _row_copy(zeros, xs_out.at[pl.ds(pl.multiple_of(b * zb, zb), zb), :], sem).wait()
            return 0

        lax.fori_loop(0, xs_out.shape[0] // zb, fill, 0)
        lax.fori_loop(0, xs_out.shape[0] // zb, done, 0)

    def issue(t, _):
        for k in range(top_k):
            _row_copy(tok_ref.at[pl.ds(t, 1), :], xs_out.at[pl.ds(dest_ref[k, t], 1), :], sem).start()
        return 0

    def drain(t, _):
        for k in range(top_k):
            _row_copy(tok_ref.at[pl.ds(t, 1), :], xs_out.at[pl.ds(dest_ref[k, t], 1), :], sem).wait()
        return 0

    lax.fori_loop(0, tb, issue, 0)
    lax.fori_loop(0, tb, drain, 0)


def _expert_body(be_ref, nu_ref, x_ref, wgu_ref, bgu_ref, wd_ref, bd_ref, o_ref, *, d_ff):
    del be_ref
    used = pl.program_id(0) < nu_ref[0]

    @pl.when(used)
    def _():
        lo, hi = _unpack_bf16_pairs(x_ref[...])
        half = lo.shape[1]
        h = _dot(lo, wgu_ref[0, :half, :]) + _dot(hi, wgu_ref[0, half:, :]) + bgu_ref[0]
        g = jnp.minimum(h[:, :d_ff], SWIGLU_LIMIT)
        u = jnp.clip(h[:, d_ff:], -SWIGLU_LIMIT, SWIGLU_LIMIT)
        act = (u + 1.0) * (g * jax.nn.sigmoid(SWIGLU_ALPHA * g))
        o_ref[...] = _dot(act.astype(BF16), wd_ref[0]) + bd_ref[0]

    @pl.when(jnp.logical_not(used))
    def _():
        o_ref[...] = jnp.zeros_like(o_ref)


def _combine_body(dest_ref, h_ref, gt_ref, nf_ref, os_hbm, y_ref, buf, sem, *, top_k):
    tb = h_ref.shape[0]

    def issue(t, _):
        for k in range(top_k):
            _row_copy(os_hbm.at[pl.ds(dest_ref[k, t], 1), :], buf.at[k, pl.ds(t, 1), :], sem).start()
        return 0

    def drain(t, _):
        for k in range(top_k):
            _row_copy(os_hbm.at[pl.ds(dest_ref[k, t], 1), :], buf.at[k, pl.ds(t, 1), :], sem).wait()
        return 0

    lax.fori_loop(0, tb, issue, 0)
    lax.fori_loop(0, tb, drain, 0)
    gt = gt_ref[...]
    f = buf[0] * gt[:, 0:1]
    for k in range(1, top_k):
        f = f + buf[k] * gt[:, k:k + 1]
    y_ref[...] = _rms(h_ref[...] + f, nf_ref[...])


def _largest_pow2_divisor(n, cap):
    t = cap
    while n % t:
        t //= 2
    return t


def _rotation_tables(pos, half):
    inv_freq = 1.0 / (ROPE_BASE ** jnp.linspace(0.0, 1.0, half, dtype=F32))
    ang = pos.astype(F32)[:, None] * inv_freq[None, :]
    cos, sin = jnp.cos(ang), jnp.sin(ang)
    return jnp.concatenate([cos, cos], axis=1), jnp.concatenate([-sin, sin], axis=1)


def _decay_tables(log_g, c):
    idx = jnp.arange(c, dtype=F32)
    lg = log_g[:, None]
    rel = idx[:, None] - idx[None, :]
    dec = jnp.where(rel >= 0, jnp.exp(lg[:, :, None] * jnp.maximum(rel, 0.0)), 0.0)
    rowdec = jnp.exp(lg * (idx + 1.0))[:, :, None]
    coldec = jnp.exp(lg * (c - 1.0 - idx))[:, :, None]
    gc = jnp.exp(lg * c)[:, :, None]
    return dec, rowdec, coldec, gc


def kernel(x_prompt, x_sample, cache_sb_k, cache_sb_v, state_ret, meta_tokens, norm_mix, w_in, w_branch_sb, w_branch_ret, w_out, norm_ffn, w_router, b_router, w_gate_up, b_gate_up, w_down, b_down, norm_final):
    B, SEQ, D = x_prompt.shape
    Bs, Ts, _ = x_sample.shape
    depth, _, past, h_sb, dh = cache_sb_k.shape
    _, _, h_ret, dk, dv = state_ret.shape
    n_meta = meta_tokens.shape[0]
    n_exp = w_router.shape[-1]
    d_ff = w_down.shape[-2]
    sbw, rqk, rv = h_sb * dh, h_ret * dk, h_ret * dv
    assert depth == 1, "one layer: the meta-token rows feed nothing after the mixers"
    assert dk == LANES and 2 * dh == LANES and h_sb * Ts == LANES and n_meta % 16 == 0

    TQ = _largest_pow2_divisor(SEQ, 512)
    assert TQ >= 2 * LANES and n_meta <= LANES
    t_p, t_s = B * SEQ, Bs * Ts
    t_moe = t_p + t_s
    TM = _largest_pow2_divisor(t_moe, 512)
    assert TM >= 16 and SEQ % TM == 0 and t_p % TM == 0 and t_moe % n_meta == 0 and t_p % Ts == 0
    t_all = -(-(t_moe + n_meta) // TM) * TM
    n_p, n_moe, n_all = t_p // TM, t_moe // TM, t_all // TM
    meta_blk = t_moe // n_meta

    x_all = jnp.concatenate([x_prompt.reshape(t_p, D), x_sample.reshape(t_s, D), meta_tokens.astype(F32),
                             jnp.zeros((t_all - t_moe - n_meta, D), F32)], axis=0)

    sizes = (sbw,) * 3 + (rqk,) * 2 + (rv,) * 2 + (D,) * 2
    offs = tuple(sum(sizes[:i]) for i in range(len(sizes) + 1))
    small_pos = jnp.concatenate([jnp.tile(past + jnp.arange(Ts), Bs), jnp.arange(n_meta) - n_meta,
                                 jnp.zeros((t_all - t_moe - n_meta,), jnp.int32)])
    cs_tab, sn_tab = _rotation_tables(jnp.concatenate([jnp.arange(SEQ), small_pos]), dk // 2)
    n_sq = SEQ // TM

    def tab_map(i):
        return (jnp.where(i < n_p, i % n_sq, n_sq + i - n_p), 0)

    row = lambda w: pl.BlockSpec((TM, w), lambda i: (i, 0))
    out_dt = (BF16, F32, F32, F32, F32, BF16, F32, F32, F32)
    qa, ka, va, qr, kr, vr, gr, ga, gb = pl.pallas_call(
        functools.partial(_inproj_body, offs=offs, q_scale=dh ** -0.5, k_scale=dk ** -0.5, n_ret=h_ret, dk=dk),
        grid=(n_all,),
        in_specs=[row(D), pl.BlockSpec((1, D), lambda i: (0, 0)),
                  pl.BlockSpec((D, offs[-1]), lambda i: (0, 0), pipeline_mode=pl.Buffered(1)),
                  pl.BlockSpec((TM, dk), tab_map), pl.BlockSpec((TM, dk), tab_map)],
        out_specs=[row(w) for w in sizes],
        out_shape=[jax.ShapeDtypeStruct((t_all, w), dt) for w, dt in zip(sizes, out_dt)],
        compiler_params=_params("parallel"), name="inproj",
    )(x_all, norm_mix[0][None, :], w_in[0].astype(BF16), cs_tab, sn_tab)

    n_q = SEQ // TQ
    n_pair = sbw // LANES
    ya_p = pl.pallas_call(
        functools.partial(_sb_prompt_body, tq=TQ, n_meta=n_meta, dh=dh),
        grid=(B, n_pair, n_q),
        in_specs=[pl.BlockSpec((TQ, LANES), lambda b, p, q: (b * n_q + q, p)),
                  pl.BlockSpec((SEQ, LANES), lambda b, p, q: (b, p)),
                  pl.BlockSpec((SEQ, LANES), lambda b, p, q: (b, p)),
                  pl.BlockSpec((n_meta, LANES), lambda b, p, q: (meta_blk, p)),
                  pl.BlockSpec((n_meta, LANES), lambda b, p, q: (meta_blk, p))],
        out_specs=pl.BlockSpec((TQ, LANES), lambda b, p, q: (b * n_q + q, p)),
        out_shape=jax.ShapeDtypeStruct((t_p, sbw), BF16),
        scratch_shapes=[pltpu.VMEM((LANES + SEQ, LANES), BF16), pltpu.VMEM((LANES + SEQ, LANES), BF16),
                        pltpu.VMEM((2 * TQ, 1), F32), pltpu.VMEM((2 * TQ, LANES), F32)],
        compiler_params=_params("parallel", "parallel", "arbitrary"), name="sb_prompt",
    )(qa, ka, va, ka, va)

    KC = _largest_pow2_divisor(past, 512)
    SUB = min(KC, 256)
    n_kc = past // KC
    smp = lambda w: pl.BlockSpec((Ts, w), lambda b, j: (t_p // Ts + b, 0))
    met = lambda w: pl.BlockSpec((n_meta, w), lambda b, j: (meta_blk, 0))
    cache = pl.BlockSpec((1, KC, sbw), lambda b, j: (b, n_kc - 1 - j, 0))
    ya_s = pl.pallas_call(
        functools.partial(_sb_sample_body, ts=Ts, n_meta=n_meta, dh=dh, sub=SUB),
        grid=(Bs, n_kc),
        in_specs=[smp(sbw), smp(sbw), smp(sbw), met(sbw), met(sbw), cache, cache],
        out_specs=pl.BlockSpec((Ts, sbw), lambda b, j: (b, 0)),
        out_shape=jax.ShapeDtypeStruct((t_s, sbw), BF16),
        scratch_shapes=[pltpu.VMEM((LANES, sbw), BF16), pltpu.VMEM((1, LANES), F32), pltpu.VMEM((LANES, sbw), F32)],
        compiler_params=_params("parallel", "arbitrary"), name="sb_sample",
    )(qa, ka, va, ka, va, cache_sb_k[0].reshape(Bs, past, sbw), cache_sb_v[0].reshape(Bs, past, sbw))

    log_g = jnp.log(1.0 - 2.0 ** (-5.0 - jnp.arange(h_ret, dtype=F32)))
    RC = _largest_pow2_divisor(SEQ, 256)
    dec, rd, cd, gc = _decay_tables(log_g, RC)
    _, _, mcd, _ = _decay_tables(log_g, n_meta)
    per_b = lambda w: pl.BlockSpec((SEQ, w), lambda b, h: (b, h))
    met_h = lambda w: pl.BlockSpec((n_meta, w), lambda b, h: (meta_blk, h))
    tab = lambda a: pl.BlockSpec((1,) + a.shape[1:], lambda b, h: (h, 0, 0))
    state_spec = pl.BlockSpec((1, 1, dk, dv), lambda b, h: (b, h, 0, 0))
    yr_p, s_p = pl.pallas_call(
        functools.partial(_ret_prompt_body, chunk=RC),
        grid=(B, h_ret),
        in_specs=[per_b(dk), per_b(dk), per_b(dv), per_b(dv), met_h(dk), met_h(dv),
                  tab(dec), tab(rd), tab(cd), tab(gc), tab(mcd)],
        out_specs=[per_b(dv), state_spec],
        out_shape=[jax.ShapeDtypeStruct((t_p, rv), BF16), jax.ShapeDtypeStruct((B, h_ret, dk, dv), F32)],
        compiler_params=_params("parallel", "parallel"), name="ret_prompt",
    )(qr, kr, vr, gr, kr, vr, dec, rd, cd, gc, mcd)

    dec_s, rd_s, cd_s, gc_s = _decay_tables(log_g, Ts)
    smp_h = lambda w: pl.BlockSpec((Ts, w), lambda b, h: (t_p // Ts + b, h))
    yr_s, s_s = pl.pallas_call(
        _ret_sample_body,
        grid=(Bs, h_ret),
        in_specs=[smp_h(dk), smp_h(dk), smp_h(dv), smp_h(dv), state_spec,
                  tab(dec_s), tab(rd_s), tab(cd_s), tab(gc_s)],
        out_specs=[pl.BlockSpec((Ts, dv), lambda b, h: (b, h)), state_spec],
        out_shape=[jax.ShapeDtypeStruct((t_s, rv), BF16), jax.ShapeDtypeStruct((Bs, h_ret, dk, dv), F32)],
        compiler_params=_params("parallel", "parallel"), name="ret_sample",
    )(qr, kr, vr, gr, state_ret[0], dec_s, rd_s, cd_s, gc_s)

    GW = 8
    const = lambda a: pl.BlockSpec(a.shape, lambda i: (0,) * a.ndim)
    prm = lambda w: pl.BlockSpec((TM, w), lambda i: (jnp.minimum(i, n_p - 1), 0))
    smp_m = lambda w: pl.BlockSpec((TM, w), lambda i: (jnp.maximum(i - n_p, 0), 0))
    wa, wb, wo = w_branch_sb[0].astype(BF16), w_branch_ret[0].astype(BF16), w_out[0].astype(BF16)
    nf, wr_t, br = norm_ffn[0][None, :], w_router[0].T.astype(BF16), b_router[0][:, None]
    h1, tok, ids, gates_t = pl.pallas_call(
        functools.partial(_merge_body, n_prompt=n_p, top_k=TOP_K),
        grid=(n_moe,),
        in_specs=[prm(sbw), prm(rv), smp_m(sbw), smp_m(rv), row(D), row(D), row(D), const(wa), const(wb),
                  const(wo), const(nf), const(wr_t), const(br)],
        out_specs=[row(D), row(D // 2), pl.BlockSpec((TOP_K, TM), lambda i: (0, i)), row(GW)],
        out_shape=[jax.ShapeDtypeStruct((t_moe, D), F32), jax.ShapeDtypeStruct((t_moe, D // 2), jnp.uint32),
                   jax.ShapeDtypeStruct((TOP_K, t_moe), jnp.int32), jax.ShapeDtypeStruct((t_moe, GW), F32)],
        compiler_params=_params("parallel"), name="merge_router",
    )(ya_p, yr_p, ya_s, yr_s, ga, gb, x_all, wa, wb, wo, nf, wr_t, br)

    rank, cnt = pl.pallas_call(
        functools.partial(_rank_body, n_exp=n_exp),
        grid=(n_moe,),
        in_specs=[pl.BlockSpec((TOP_K, TM), lambda i: (0, i))],
        out_specs=[pl.BlockSpec((TOP_K, TM), lambda i: (0, i)), pl.BlockSpec((n_exp, 1), lambda i: (0, 0))],
        out_shape=[jax.ShapeDtypeStruct((TOP_K, t_moe), jnp.int32), jax.ShapeDtypeStruct((n_exp, 1), F32)],
        scratch_shapes=[pltpu.VMEM((n_exp, 1), F32)],
        compiler_params=_params("arbitrary"), name="moe_rank",
    )(ids)
    BM = TM
    counts = cnt[:, 0].astype(jnp.int32)
    padded = (counts + BM - 1) // BM * BM
    pad_end = jnp.cumsum(padded)
    pad_start = pad_end - padded
    onehot = ids[None] == jnp.arange(n_exp, dtype=jnp.int32)[:, None, None]
    dest = jnp.sum(jnp.where(onehot, pad_start[:, None, None], 0), axis=0) + rank
    n_blk = -(-(t_moe * TOP_K) // BM) + n_exp
    block_e = jnp.minimum(jnp.sum(pad_end[None, :] <= (jnp.arange(n_blk) * BM)[:, None], axis=1), n_exp - 1).astype(jnp.int32)
    n_used = (pad_end[-1:] // BM).astype(jnp.int32)

    TB = min(TM, 256)
    dest_spec = lambda off: pl.BlockSpec((TOP_K, TB), lambda i: (0, i + off), memory_space=pltpu.SMEM)
    xs = pl.pallas_call(
        functools.partial(_dispatch_body, top_k=TOP_K),
        grid=(t_moe // TB,),
        in_specs=[dest_spec(0), pl.BlockSpec((TB, D // 2), lambda i: (i, 0))],
        out_specs=pl.BlockSpec(memory_space=pl.ANY),
        out_shape=jax.ShapeDtypeStruct((n_blk * BM, D // 2), jnp.uint32),
        scratch_shapes=[pltpu.VMEM((BM, D // 2), jnp.uint32), pltpu.SemaphoreType.DMA],
        compiler_params=_params("arbitrary"), name="moe_dispatch",
    )(dest, tok)

    blk = lambda i, be, nu: jnp.minimum(i, nu[0] - 1)
    exp_w = lambda a: pl.BlockSpec((1,) + a.shape[1:], lambda i, be, nu: (be[blk(i, be, nu)],) + (0,) * (a.ndim - 1))
    wgu, bgu = w_gate_up[0].astype(BF16), b_gate_up[0][:, None, :]
    wd, bd = w_down[0].astype(BF16), b_down[0][:, None, :]
    os_ = pl.pallas_call(
        functools.partial(_expert_body, d_ff=d_ff),
        grid_spec=pltpu.PrefetchScalarGridSpec(
            num_scalar_prefetch=2, grid=(n_blk,),
            in_specs=[pl.BlockSpec((BM, D // 2), lambda i, be, nu: (blk(i, be, nu), 0)),
                      exp_w(wgu), exp_w(bgu), exp_w(wd), exp_w(bd)],
            out_specs=pl.BlockSpec((BM, D), lambda i, be, nu: (i, 0))),
        out_shape=jax.ShapeDtypeStruct((n_blk * BM, D), F32),
        compiler_params=_params("arbitrary"), name="moe_experts",
    )(block_e, n_used, xs, wgu, bgu, wd, bd)

    def combine(off, n):
        return pl.pallas_call(
            functools.partial(_combine_body, top_k=TOP_K),
            grid=(n,),
            in_specs=[dest_spec(off), pl.BlockSpec((TB, D), lambda i: (i + off, 0)),
                      pl.BlockSpec((TB, GW), lambda i: (i + off, 0)), pl.BlockSpec((1, D), lambda i: (0, 0)),
                      pl.BlockSpec(memory_space=pl.ANY)],
            out_specs=pl.BlockSpec((TB, D), lambda i: (i, 0)),
            out_shape=jax.ShapeDtypeStruct((n * TB, D), F32),
            scratch_shapes=[pltpu.VMEM((TOP_K, TB, D), F32), pltpu.SemaphoreType.DMA],
            compiler_params=_params("arbitrary"), name="moe_combine",
        )(dest, h1, gates_t, norm_final[None, :], os_)

    y_prompt = combine(0, t_p // TB).reshape(B, SEQ, D)
    y_sample = combine(t_p // TB, t_s // TB).reshape(Bs, Ts, D)

    def with_meta(a):
        meta_rows = jnp.broadcast_to(a[t_moe:t_moe + n_meta][None], (B, n_meta, sbw))
        return jnp.concatenate([meta_rows, a[:t_p].reshape(B, SEQ, sbw)], axis=1).reshape(1, B, n_meta + SEQ, h_sb, dh)

    return (y_prompt, y_sample, with_meta(ka), with_meta(va), s_p[None],
            ka[t_p:t_moe].reshape(1, Bs, Ts, h_sb, dh), va[t_p:t_moe].reshape(1, Bs, Ts, h_sb, dh), s_s[None])
```

```python
import functools

import jax
import jax.numpy as jnp
from jax import lax
from jax.experimental import pallas as pl
from jax.experimental.pallas import tpu as pltpu

F32 = jnp.float32
BF16 = jnp.bfloat16

EPS = 1e-6
ROPE_BASE = 10000.0
TOP_K = 4
SWIGLU_LIMIT = 7.0
SWIGLU_ALPHA = 1.702

LANES = 128
SEG_ALIGN = 8
VMEM_LIMIT = 56 << 20

_NT = (((1,), (1,)), ((), ()))
_TN = (((0,), (0,)), ((), ()))


def _params(*sem):
    return pltpu.CompilerParams(dimension_semantics=sem, vmem_limit_bytes=VMEM_LIMIT)


def _dot(a, b):
    return jnp.dot(a, b, preferred_element_type=F32)


def _rms(x, g):
    return x * lax.rsqrt(jnp.mean(x * x, axis=-1, keepdims=True) + EPS) * g


def _inproj_body(x_ref, g_ref, w_ref, cs_ref, sn_ref, qa_ref, ka_ref, va_ref, qr_ref, kr_ref,
                 vr_ref, gr_ref, ga_ref, gb_ref, *, offs, q_scale, k_scale, n_ret, dk):
    xn = _rms(x_ref[...], g_ref[...]).astype(BF16)

    def proj(i):
        return _dot(xn, w_ref[:, offs[i]:offs[i + 1]])

    qa_ref[...] = (proj(0) * q_scale).astype(BF16)
    ka_ref[...] = proj(1)
    va_ref[...] = proj(2)
    cs = cs_ref[...]
    sn = sn_ref[...]
    for ref, i, scale in ((qr_ref, 3, None), (kr_ref, 4, k_scale)):
        t = proj(i)
        for h in range(n_ret):
            th = t[:, h * dk:(h + 1) * dk]
            r = th * cs + pltpu.roll(th, dk // 2, axis=1) * sn
            ref[:, h * dk:(h + 1) * dk] = r if scale is None else r * scale
    vr_ref[...] = proj(5).astype(BF16)
    gr_ref[...] = proj(6)
    ga_ref[...] = proj(7)
    gb_ref[...] = proj(8)


def _sb_scores(z):
    log_beta = jnp.minimum(z, 0.0) - jnp.log(1.0 + jnp.exp(-jnp.abs(z)))
    return log_beta, log_beta - z


def _split_bf16(x):
    hi = x.astype(BF16)
    return hi, (x - hi.astype(F32)).astype(BF16)


def _suffix_ones(n):
    r = lax.broadcasted_iota(jnp.int32, (2 * n, n), 0) & (n - 1)
    c = lax.broadcasted_iota(jnp.int32, (2 * n, n), 1)
    return jnp.where(r >= c, 1.0, 0.0).astype(BF16)


def _sb_block(qs, kblk, vblk, suffix2, mask, carry, acc):
    z = lax.dot_general(qs, kblk, _NT, preferred_element_type=F32)
    _, log_keep = _sb_scores(z)
    if mask is not None:
        log_keep = jnp.where(mask, log_keep, 0.0)
    hi, lo = _split_bf16(log_keep)
    incl = _dot(jnp.concatenate([hi, lo], axis=1), suffix2)
    w = jnp.exp(z + (incl + carry))
    if mask is not None:
        w = jnp.where(mask, w, 0.0)
    return carry + incl[:, 0:1], acc + _dot(w.astype(BF16), vblk)


def _sb_prompt_body(q_ref, k_ref, v_ref, mk_ref, mv_ref, o_ref, kb, vb, carry_ref, acc_ref, *, tq, n_meta, dh):
    qi = pl.program_id(2)
    tk = tq // 2
    mpad = LANES - n_meta
    n_heads = LANES // dh

    @pl.when(qi == 0)
    def _():
        kb[0:mpad, :] = jnp.zeros((mpad, LANES), BF16)
        vb[0:mpad, :] = jnp.zeros((mpad, LANES), BF16)
        kb[mpad:LANES, :] = mk_ref[...].astype(BF16)
        vb[mpad:LANES, :] = mv_ref[...].astype(BF16)
        kb[LANES:, :] = k_ref[...].astype(BF16)
        vb[LANES:, :] = v_ref[...].astype(BF16)

    q = q_ref[...]
    lane = lax.broadcasted_iota(jnp.int32, (tk, LANES), 1)
    parts = []
    for half in range(2):
        qh = q[half * tk:(half + 1) * tk]
        for head in range(n_heads):
            parts.append(jnp.where(lane // dh == head, qh, jnp.zeros_like(qh)))
    qs = jnp.concatenate(parts, axis=0)
    m_half = n_heads * tk
    suffix2 = _suffix_ones(tk)
    base = pl.multiple_of(LANES + qi * tq, LANES)

    r = lax.broadcasted_iota(jnp.int32, (m_half, tk), 0) & (tk - 1)
    c = lax.broadcasted_iota(jnp.int32, (m_half, tk), 1)
    carry_b, acc_b = _sb_block(qs[m_half:], kb[pl.ds(base + tk, tk), :], vb[pl.ds(base + tk, tk), :], suffix2,
                               c < r, jnp.zeros((m_half, 1), F32), jnp.zeros((m_half, LANES), F32))
    carry = jnp.concatenate([jnp.zeros((m_half, 1), F32), carry_b], axis=0)
    acc = jnp.concatenate([jnp.zeros((m_half, LANES), F32), acc_b], axis=0)
    r = lax.broadcasted_iota(jnp.int32, (2 * m_half, tk), 0)
    c = lax.broadcasted_iota(jnp.int32, (2 * m_half, tk), 1)
    carry, acc = _sb_block(qs, kb[pl.ds(base, tk), :], vb[pl.ds(base, tk), :], suffix2,
                           (r >= m_half) | (c < (r & (tk - 1))), carry, acc)
    carry_ref[...] = carry
    acc_ref[...] = acc

    def left(it, _):
        off = pl.multiple_of(base - (it + 1) * tk, LANES)
        carry, acc = _sb_block(qs, kb[pl.ds(off, tk), :], vb[pl.ds(off, tk), :], suffix2, None,
                               carry_ref[...], acc_ref[...])
        carry_ref[...] = carry
        acc_ref[...] = acc
        return 0

    lax.fori_loop(0, 2 * qi, left, 0)
    c = lax.broadcasted_iota(jnp.int32, (2 * m_half, LANES), 1)
    _, acc = _sb_block(qs, kb[0:LANES, :], vb[0:LANES, :], _suffix_ones(LANES), c >= mpad,
                       carry_ref[...], acc_ref[...])
    for half in range(2):
        for head in range(n_heads):
            rows = (half * n_heads + head) * tk
            o_ref[half * tk:(half + 1) * tk, head * dh:(head + 1) * dh] = (
                acc[rows:rows + tk, head * dh:(head + 1) * dh].astype(o_ref.dtype))


def _sb_sample_body(q_ref, nk_ref, nv_ref, mk_ref, mv_ref, ck_ref, cv_ref, o_ref,
                    qbd, carry_ref, acc_ref, *, ts, n_meta, dh, sub):
    j = pl.program_id(1)
    width = q_ref.shape[1]
    n_heads = width // dh

    def head_mask(shape):
        r = lax.broadcasted_iota(jnp.int32, shape, 0)
        c = lax.broadcasted_iota(jnp.int32, shape, 1)
        return r // ts == c // dh

    def sweep(k, v, mask, carry, acc):
        n = k.shape[0]
        z = lax.dot_general(k.astype(BF16), qbd[...], _NT, preferred_element_type=F32)
        log_beta, log_keep = _sb_scores(z)
        if mask is not None:
            log_keep = jnp.where(mask, log_keep, 0.0)
        r = lax.broadcasted_iota(jnp.int32, (n, n), 0)
        c = lax.broadcasted_iota(jnp.int32, (n, n), 1)
        suffix = jnp.where(c >= r, 1.0, 0.0).astype(BF16)
        hi, lo = _split_bf16(log_keep)
        incl = _dot(suffix, hi) + _dot(suffix, lo)
        w = jnp.exp(log_beta + (carry + incl - log_keep))
        if mask is not None:
            w = jnp.where(mask, w, 0.0)
        acc = acc + lax.dot_general(w.astype(BF16), v.astype(BF16), _TN, preferred_element_type=F32)
        return carry + incl[0:1, :], acc

    @pl.when(j == 0)
    def _():
        qt = jnp.concatenate([q_ref[...]] * n_heads, axis=0)
        qbd[...] = jnp.where(head_mask((n_heads * ts, width)), qt, jnp.zeros_like(qt))
        key = lax.broadcasted_iota(jnp.int32, (ts, LANES), 0)
        qry = lax.broadcasted_iota(jnp.int32, (ts, LANES), 1) % ts
        carry, acc = sweep(nk_ref[...], nv_ref[...], key < qry,
                           jnp.zeros((1, LANES), F32), jnp.zeros((LANES, width), F32))
        carry_ref[...] = carry
        acc_ref[...] = acc

    carry = carry_ref[...]
    acc = acc_ref[...]
    chunk = ck_ref.shape[1]
    for s in range(chunk // sub - 1, -1, -1):
        carry, acc = sweep(ck_ref[0, s * sub:(s + 1) * sub, :], cv_ref[0, s * sub:(s + 1) * sub, :],
                           None, carry, acc)
    carry_ref[...] = carry
    acc_ref[...] = acc

    @pl.when(j == pl.num_programs(1) - 1)
    def _():
        _, acc = sweep(mk_ref[...], mv_ref[...], None, carry_ref[...], acc_ref[...])
        own = jnp.where(head_mask((n_heads * ts, width)), acc, 0.0)
        o_ref[...] = jnp.sum(own.reshape(n_heads, ts, width), axis=0).astype(o_ref.dtype)


def _ret_chunk(q, k, v, state, dec, rowdec, coldec, gc):
    qb = q.astype(BF16)
    inner = lax.dot_general(qb, k.astype(BF16), _NT, preferred_element_type=F32) * dec
    o = _dot(inner.astype(BF16), v) + _dot(qb, state.astype(BF16)) * rowdec
    kw = (k * coldec).astype(BF16)
    return gc * state + lax.dot_general(kw, v, _TN, preferred_element_type=F32), o


def _ret_gate(o, g):
    on = o * lax.rsqrt(jnp.mean(o * o, axis=-1, keepdims=True) + EPS)
    return (g * jax.nn.sigmoid(g)) * on


def _ret_prompt_body(q_ref, k_ref, v_ref, g_ref, mk_ref, mv_ref, dec_ref, rd_ref, cd_ref, gc_ref,
                     mcd_ref, y_ref, s_ref, *, chunk):
    kw = (mk_ref[...] * mcd_ref[0]).astype(BF16)
    state = lax.dot_general(kw, mv_ref[...], _TN, preferred_element_type=F32)
    dec, rd, cd, gc = dec_ref[0], rd_ref[0], cd_ref[0], gc_ref[0]

    def step(c, state):
        r = pl.ds(pl.multiple_of(c * chunk, chunk), chunk)
        state, o = _ret_chunk(q_ref[r, :], k_ref[r, :], v_ref[r, :], state, dec, rd, cd, gc)
        y_ref[r, :] = _ret_gate(o, g_ref[r, :]).astype(y_ref.dtype)
        return state

    s_ref[0, 0] = lax.fori_loop(0, q_ref.shape[0] // chunk, step, state)


def _ret_sample_body(q_ref, k_ref, v_ref, g_ref, s0_ref, dec_ref, rd_ref, cd_ref, gc_ref,
                     y_ref, s_ref):
    state, o = _ret_chunk(q_ref[...], k_ref[...], v_ref[...], s0_ref[0, 0], dec_ref[0], rd_ref[0],
                          cd_ref[0], gc_ref[0])
    y_ref[...] = _ret_gate(o, g_ref[...]).astype(y_ref.dtype)
    s_ref[0, 0] = state


def _pack_bf16_pairs(x):
    bits = lax.bitcast_convert_type(x.astype(BF16).astype(F32), jnp.uint32)
    w = x.shape[1] // 2
    return (bits[:, :w] >> 16) | bits[:, w:]


def _unpack_bf16_pairs(p):
    lo = lax.bitcast_convert_type(p << 16, F32).astype(BF16)
    hi = lax.bitcast_convert_type(p & jnp.uint32(0xFFFF0000), F32).astype(BF16)
    return lo, hi


def _merge_body(*refs, n_prompt, top_k):
    prompt, sample, rest = refs[0:5], refs[5:10], refs[10:]
    i = pl.program_id(0)

    @pl.when(i < n_prompt)
    def _():
        _merge_block(*prompt, *rest, top_k=top_k)

    @pl.when(i >= n_prompt)
    def _():
        _merge_block(*sample, *rest, top_k=top_k)


def _merge_block(ya_ref, yr_ref, ga_ref, gb_ref, x_ref, wa_ref, wb_ref, wo_ref, nf_ref, wr_ref, br_ref,
                 h_ref, tok_ref, ids_ref, gt_ref, cnt_ref, *, top_k):
    mix = (jax.nn.sigmoid(ga_ref[...]) * _dot(ya_ref[...], wa_ref[...])
           + jax.nn.sigmoid(gb_ref[...]) * _dot(yr_ref[...], wb_ref[...]))
    h = x_ref[...] + _dot(mix.astype(BF16), wo_ref[...])
    h_ref[...] = h
    tok = _rms(h, nf_ref[...])
    tok_ref[...] = _pack_bf16_pairs(tok)
    logits = lax.dot_general(wr_ref[...], tok.astype(BF16), _NT, preferred_element_type=F32) + br_ref[...]
    n_exp, tm = logits.shape
    eidx = lax.broadcasted_iota(jnp.int32, (n_exp, tm), 0).astype(F32)
    ids, tops = [], []
    for _ in range(top_k):
        m = jnp.max(logits, axis=0, keepdims=True)
        sel = jnp.min(jnp.where(logits == m, eidx, float(n_exp)), axis=0, keepdims=True)
        ids.append(sel)
        tops.append(m)
        logits = jnp.where(eidx == sel, -jnp.inf, logits)
    ex = [jnp.exp(t - tops[0]) for t in tops]
    inv = 1.0 / functools.reduce(lambda a, b: a + b, ex)
    ids_ref[...] = jnp.concatenate(ids, axis=0).astype(jnp.int32)
    gt_ref[...] = jnp.concatenate([e * inv for e in ex], axis=0)
    cnt = jnp.zeros((n_exp, 1), F32)
    for sel in ids:
        cnt = cnt + jnp.sum(jnp.where(eidx == sel, 1.0, 0.0), axis=1, keepdims=True)
    cnt_ref[0] = cnt


def _local_positions(ids_ref, seg_col, n_exp):
    top_k, tb = ids_ref.shape
    ids = ids_ref[...].astype(F32)
    eidx = lax.broadcasted_iota(jnp.int32, (n_exp, tb), 0).astype(F32)
    r = lax.broadcasted_iota(jnp.int32, (tb, tb), 0)
    c = lax.broadcasted_iota(jnp.int32, (tb, tb), 1)
    before = jnp.where(r < c, 1.0, 0.0).astype(BF16)
    run = seg_col
    rows = []
    for k in range(top_k):
        onehot = jnp.where(ids[k:k + 1, :] == eidx, 1.0, 0.0)
        pos = _dot(onehot.astype(BF16), before) + run
        rows.append(jnp.sum(onehot * pos, axis=0, keepdims=True))
        run = run + jnp.sum(onehot, axis=1, keepdims=True)
    return jnp.concatenate(rows, axis=0)


def _segment_copies(cnt_ref, seg_ref, row_ref, blk, n_exp, max_rows, vmem, hbm, sem, *, to_hbm, wait):
    def per_expert(e, _):
        n = cnt_ref[blk * n_exp + e]
        src0 = seg_ref[blk * n_exp + e]
        dst0 = row_ref[blk * n_exp + e]
        off = jnp.int32(0)
        for bit in range(max_rows.bit_length() - 1, SEG_ALIGN.bit_length() - 2, -1):
            size = 1 << bit
            take = (n >> bit) & 1

            @pl.when(take == 1)
            def _(off=off, size=size):
                v = vmem.at[pl.ds(pl.multiple_of(src0 + off, SEG_ALIGN), size), :]
                h = hbm.at[pl.ds(pl.multiple_of(dst0 + off, SEG_ALIGN), size), :]
                cp = pltpu.make_async_copy(v, h, sem) if to_hbm else pltpu.make_async_copy(h, v, sem)
                if wait:
                    cp.wait()
                else:
                    cp.start()

            off = off + take * size
        return 0

    lax.fori_loop(0, n_exp, per_expert, 0)


def _permutation(pos, n_rows):
    p = lax.broadcasted_iota(jnp.int32, (n_rows, pos.shape[1]), 0)
    out = jnp.where(pos[0:1, :] == p, 1.0, 0.0)
    for k in range(1, pos.shape[0]):
        out = out + jnp.where(pos[k:k + 1, :] == p, 1.0, 0.0)
    return out


def _dispatch_body(cnt_ref, seg_ref, row_ref, ids_ref, gt_ref, segc_ref, tok_ref, xs_out, pos_ref,
                   loc, zeros, sem, *, n_exp):
    blk = pl.program_id(0)
    top_k, tb = ids_ref.shape
    zb = zeros.shape[0]
    words = tok_ref.shape[1]

    @pl.when(blk == 0)
    def _():
        zeros[...] = jnp.zeros_like(zeros)

        def fill(b, _):
            pltpu.make_async_copy(zeros, xs_out.at[pl.ds(pl.multiple_of(b * zb, zb), zb), :], sem).start()
            return 0

        def done(b, _):
            pltpu.make_async_copy(zeros, xs_out.at[pl.ds(pl.multiple_of(b * zb, zb), zb), :], sem).wait()
            return 0

        lax.fori_loop(0, xs_out.shape[0] // zb, fill, 0)
        lax.fori_loop(0, xs_out.shape[0] // zb, done, 0)

    pos = _local_positions(ids_ref, segc_ref[0], n_exp).astype(jnp.int32)
    pos_ref[...] = pos
    n_rows = loc.shape[0]
    perm = _permutation(pos, n_rows).astype(BF16)
    lo, hi = _unpack_bf16_pairs(tok_ref[...])
    lo_bits = lax.bitcast_convert_type(_dot(perm, lo), jnp.uint32)
    hi_bits = lax.bitcast_convert_type(_dot(perm, hi), jnp.uint32)
    loc[:, 0:words] = (lo_bits >> 16) | hi_bits
    p = lax.broadcasted_iota(jnp.int32, (n_rows, tb), 0)
    gates = gt_ref[...]
    spread = jnp.where(pos[0:1, :] == p, gates[0:1, :], 0.0)
    for k in range(1, top_k):
        spread = spread + jnp.where(pos[k:k + 1, :] == p, gates[k:k + 1, :], 0.0)
    gate_row = jnp.sum(spread, axis=1, keepdims=True)
    loc[:, words:] = lax.bitcast_convert_type(jnp.broadcast_to(gate_row, (n_rows, LANES)), jnp.uint32)
    copies = functools.partial(_segment_copies, cnt_ref, seg_ref, row_ref, blk, n_exp, n_rows, loc, xs_out, sem,
                               to_hbm=True)
    copies(wait=False)
    copies(wait=True)


def _expert_body(be_ref, nu_ref, x_ref, wgu_ref, bgu_ref, wd_ref, bd_ref, o_ref, *, d_ff):
    del be_ref
    used = pl.program_id(0) < nu_ref[0]

    @pl.when(used)
    def _():
        x = x_ref[...]
        half = x.shape[1] - LANES
        lo, hi = _unpack_bf16_pairs(x[:, :half])
        gate = lax.bitcast_convert_type(x[:, half:half + 1], F32)
        h = _dot(lo, wgu_ref[0, :half, :]) + _dot(hi, wgu_ref[0, half:, :]) + bgu_ref[0]
        g = jnp.minimum(h[:, :d_ff], SWIGLU_LIMIT)
        u = jnp.clip(h[:, d_ff:], -SWIGLU_LIMIT, SWIGLU_LIMIT)
        act = (u + 1.0) * (g * jax.nn.sigmoid(SWIGLU_ALPHA * g))
        out = (_dot(act.astype(BF16), wd_ref[0]) + bd_ref[0]) * gate
        hi_part, lo_part = _split_bf16(out)
        o_ref[...] = (lax.bitcast_convert_type(hi_part.astype(F32), jnp.uint32)
                      | (lax.bitcast_convert_type(lo_part.astype(F32), jnp.uint32) >> 16))

    @pl.when(jnp.logical_not(used))
    def _():
        o_ref[...] = jnp.zeros_like(o_ref)


def _combine_body(cnt_ref, seg_ref, row_ref, pos_ref, h_ref, nf_ref, os_hbm, y_ref, loc, sem, *, n_exp, blk_off):
    blk = pl.program_id(0) + blk_off
    n_rows = loc.shape[0]
    copies = functools.partial(_segment_copies, cnt_ref, seg_ref, row_ref, blk, n_exp, n_rows, loc, os_hbm, sem,
                               to_hbm=False)
    n_pairs = pos_ref.shape[0] * pos_ref.shape[1]
    loc[n_pairs:, :] = jnp.zeros((n_rows - n_pairs, loc.shape[1]), loc.dtype)
    copies(wait=False)
    perm = _permutation(pos_ref[...], n_rows).astype(BF16)
    copies(wait=True)
    w = loc[...]
    hi_part = lax.bitcast_convert_type(w & jnp.uint32(0xFFFF0000), F32).astype(BF16)
    lo_part = lax.bitcast_convert_type(w << 16, F32).astype(BF16)
    f = (lax.dot_general(perm, hi_part, _TN, preferred_element_type=F32)
         + lax.dot_general(perm, lo_part, _TN, preferred_element_type=F32))
    y_ref[...] = _rms(h_ref[...] + f, nf_ref[...])


def _largest_pow2_divisor(n, cap):
    t = cap
    while n % t:
        t //= 2
    return t


def _rotation_tables(pos, half):
    inv_freq = 1.0 / (ROPE_BASE ** jnp.linspace(0.0, 1.0, half, dtype=F32))
    ang = pos.astype(F32)[:, None] * inv_freq[None, :]
    cos, sin = jnp.cos(ang), jnp.sin(ang)
    return jnp.concatenate([cos, cos], axis=1), jnp.concatenate([-sin, sin], axis=1)


def _decay_tables(log_g, c):
    idx = jnp.arange(c, dtype=F32)
    lg = log_g[:, None]
    rel = idx[:, None] - idx[None, :]
    dec = jnp.where(rel >= 0, jnp.exp(lg[:, :, None] * jnp.maximum(rel, 0.0)), 0.0)
    rowdec = jnp.exp(lg * (idx + 1.0))[:, :, None]
    coldec = jnp.exp(lg * (c - 1.0 - idx))[:, :, None]
    gc = jnp.exp(lg * c)[:, :, None]
    return dec, rowdec, coldec, gc


def kernel(x_prompt, x_sample, cache_sb_k, cache_sb_v, state_ret, meta_tokens, norm_mix, w_in, w_branch_sb, w_branch_ret, w_out, norm_ffn, w_router, b_router, w_gate_up, b_gate_up, w_down, b_down, norm_final):
    B, SEQ, D = x_prompt.shape
    Bs, Ts, _ = x_sample.shape
    depth, _, past, h_sb, dh = cache_sb_k.shape
    _, _, h_ret, dk, dv = state_ret.shape
    n_meta = meta_tokens.shape[0]
    n_exp = w_router.shape[-1]
    d_ff = w_down.shape[-2]
    sbw, rqk, rv = h_sb * dh, h_ret * dk, h_ret * dv
    assert depth == 1, "one layer: the meta-token rows feed nothing after the mixers"
    assert dk == LANES and 2 * dh == LANES and h_sb * Ts == LANES and n_meta % 16 == 0

    TQ = _largest_pow2_divisor(SEQ, 512)
    assert TQ >= 2 * LANES and n_meta <= LANES
    t_p, t_s = B * SEQ, Bs * Ts
    t_moe = t_p + t_s
    TM = _largest_pow2_divisor(t_moe, 512)
    assert TM >= 16 and SEQ % TM == 0 and t_p % TM == 0 and t_moe % n_meta == 0 and t_p % Ts == 0
    n_p, n_moe = t_p // TM, t_moe // TM
    t_x = t_s + n_meta
    meta_blk = t_s // n_meta

    x_p = x_prompt.reshape(t_p, D)
    x_x = jnp.concatenate([x_sample.reshape(t_s, D), meta_tokens.astype(F32)], axis=0)

    sizes = (sbw,) * 3 + (rqk,) * 2 + (rv,) * 2 + (D,) * 2
    offs = tuple(sum(sizes[:i]) for i in range(len(sizes) + 1))
    n_sq = SEQ // TM
    row = lambda w: pl.BlockSpec((TM, w), lambda i: (i, 0))
    out_dt = (BF16, F32, F32, F32, F32, BF16, F32, F32, F32)
    w_in_b, g_mix = w_in[0].astype(BF16), norm_mix[0][None, :]

    def inproj(x, pos, tm, tab_period):
        rows = x.shape[0]
        blk = lambda w: pl.BlockSpec((tm, w), lambda i: (i, 0))
        tab = pl.BlockSpec((tm, dk), lambda i: (i % tab_period, 0))
        return pl.pallas_call(
            functools.partial(_inproj_body, offs=offs, q_scale=dh ** -0.5, k_scale=dk ** -0.5, n_ret=h_ret, dk=dk),
            grid=(rows // tm,),
            in_specs=[blk(D), pl.BlockSpec((1, D), lambda i: (0, 0)),
                      pl.BlockSpec((D, offs[-1]), lambda i: (0, 0), pipeline_mode=pl.Buffered(1)), tab, tab],
            out_specs=[blk(w) for w in sizes],
            out_shape=[jax.ShapeDtypeStruct((rows, w), dt) for w, dt in zip(sizes, out_dt)],
            compiler_params=_params("parallel"), name="inproj",
        )(x, g_mix, w_in_b, *_rotation_tables(pos, dk // 2))

    qa, ka, va, qr, kr, vr, gr, ga, gb = inproj(x_p, jnp.arange(SEQ), TM, n_sq)
    x_pos = jnp.concatenate([jnp.tile(past + jnp.arange(Ts), Bs), jnp.arange(n_meta) - n_meta])
    qa_x, ka_x, va_x, qr_x, kr_x, vr_x, gr_x, ga_x, gb_x = inproj(x_x, x_pos, t_x, 1)

    n_q = SEQ // TQ
    n_pair = sbw // LANES
    ya_p = pl.pallas_call(
        functools.partial(_sb_prompt_body, tq=TQ, n_meta=n_meta, dh=dh),
        grid=(B, n_pair, n_q),
        in_specs=[pl.BlockSpec((TQ, LANES), lambda b, p, q: (b * n_q + q, p)),
                  pl.BlockSpec((SEQ, LANES), lambda b, p, q: (b, p)),
                  pl.BlockSpec((SEQ, LANES), lambda b, p, q: (b, p)),
                  pl.BlockSpec((n_meta, LANES), lambda b, p, q: (meta_blk, p)),
                  pl.BlockSpec((n_meta, LANES), lambda b, p, q: (meta_blk, p))],
        out_specs=pl.BlockSpec((TQ, LANES), lambda b, p, q: (b * n_q + q, p)),
        out_shape=jax.ShapeDtypeStruct((t_p, sbw), BF16),
        scratch_shapes=[pltpu.VMEM((LANES + SEQ, LANES), BF16), pltpu.VMEM((LANES + SEQ, LANES), BF16),
                        pltpu.VMEM((2 * TQ, 1), F32), pltpu.VMEM((2 * TQ, LANES), F32)],
        compiler_params=_params("parallel", "parallel", "arbitrary"), name="sb_prompt",
    )(qa, ka, va, ka_x, va_x)

    KC = _largest_pow2_divisor(past, 512)
    SUB = min(KC, 256)
    n_kc = past // KC
    smp = lambda w: pl.BlockSpec((Ts, w), lambda b, j: (b, 0))
    met = lambda w: pl.BlockSpec((n_meta, w), lambda b, j: (meta_blk, 0))
    cache = pl.BlockSpec((1, KC, sbw), lambda b, j: (b, n_kc - 1 - j, 0))
    ya_s = pl.pallas_call(
        functools.partial(_sb_sample_body, ts=Ts, n_meta=n_meta, dh=dh, sub=SUB),
        grid=(Bs, n_kc),
        in_specs=[smp(sbw), smp(sbw), smp(sbw), met(sbw), met(sbw), cache, cache],
        out_specs=pl.BlockSpec((Ts, sbw), lambda b, j: (b, 0)),
        out_shape=jax.ShapeDtypeStruct((t_s, sbw), BF16),
        scratch_shapes=[pltpu.VMEM((LANES, sbw), BF16), pltpu.VMEM((1, LANES), F32), pltpu.VMEM((LANES, sbw), F32)],
        compiler_params=_params("parallel", "arbitrary"), name="sb_sample",
    )(qa_x, ka_x, va_x, ka_x, va_x, cache_sb_k[0].reshape(Bs, past, sbw), cache_sb_v[0].reshape(Bs, past, sbw))

    log_g = jnp.log(1.0 - 2.0 ** (-5.0 - jnp.arange(h_ret, dtype=F32)))
    RC = _largest_pow2_divisor(SEQ, 256)
    dec, rd, cd, gc = _decay_tables(log_g, RC)
    _, _, mcd, _ = _decay_tables(log_g, n_meta)
    per_b = lambda w: pl.BlockSpec((SEQ, w), lambda b, h: (b, h))
    met_h = lambda w: pl.BlockSpec((n_meta, w), lambda b, h: (meta_blk, h))
    tab = lambda a: pl.BlockSpec((1,) + a.shape[1:], lambda b, h: (h, 0, 0))
    state_spec = pl.BlockSpec((1, 1, dk, dv), lambda b, h: (b, h, 0, 0))
    yr_p, s_p = pl.pallas_call(
        functools.partial(_ret_prompt_body, chunk=RC),
        grid=(B, h_ret),
        in_specs=[per_b(dk), per_b(dk), per_b(dv), per_b(dv), met_h(dk), met_h(dv),
                  tab(dec), tab(rd), tab(cd), tab(gc), tab(mcd)],
        out_specs=[per_b(dv), state_spec],
        out_shape=[jax.ShapeDtypeStruct((t_p, rv), BF16), jax.ShapeDtypeStruct((B, h_ret, dk, dv), F32)],
        compiler_params=_params("parallel", "parallel"), name="ret_prompt",
    )(qr, kr, vr, gr, kr_x, vr_x, dec, rd, cd, gc, mcd)

    dec_s, rd_s, cd_s, gc_s = _decay_tables(log_g, Ts)
    smp_h = lambda w: pl.BlockSpec((Ts, w), lambda b, h: (b, h))
    yr_s, s_s = pl.pallas_call(
        _ret_sample_body,
        grid=(Bs, h_ret),
        in_specs=[smp_h(dk), smp_h(dk), smp_h(dv), smp_h(dv), state_spec,
                  tab(dec_s), tab(rd_s), tab(cd_s), tab(gc_s)],
        out_specs=[smp_h(dv), state_spec],
        out_shape=[jax.ShapeDtypeStruct((t_s, rv), BF16), jax.ShapeDtypeStruct((Bs, h_ret, dk, dv), F32)],
        compiler_params=_params("parallel", "parallel"), name="ret_sample",
    )(qr_x, kr_x, vr_x, gr_x, state_ret[0], dec_s, rd_s, cd_s, gc_s)

    const = lambda a: pl.BlockSpec(a.shape, lambda i: (0,) * a.ndim)
    prm = lambda w: pl.BlockSpec((TM, w), lambda i: (jnp.minimum(i, n_p - 1), 0))
    smp_m = lambda w: pl.BlockSpec((TM, w), lambda i: (jnp.maximum(i - n_p, 0), 0))
    lane_major = pl.BlockSpec((TOP_K, TM), lambda i: (0, i))
    wa, wb, wo = w_branch_sb[0].astype(BF16), w_branch_ret[0].astype(BF16), w_out[0].astype(BF16)
    nf, wr_t, br = norm_ffn[0][None, :], w_router[0].T.astype(BF16), b_router[0][:, None]
    h1, tok, ids, gates, cnt = pl.pallas_call(
        functools.partial(_merge_body, n_prompt=n_p, top_k=TOP_K),
        grid=(n_moe,),
        in_specs=[prm(sbw), prm(rv), prm(D), prm(D), prm(D), smp_m(sbw), smp_m(rv), smp_m(D), smp_m(D), smp_m(D),
                  const(wa), const(wb), const(wo), const(nf), const(wr_t), const(br)],
        out_specs=[row(D), row(D // 2), lane_major, lane_major, pl.BlockSpec((1, n_exp, 1), lambda i: (i, 0, 0))],
        out_shape=[jax.ShapeDtypeStruct((t_moe, D), F32), jax.ShapeDtypeStruct((t_moe, D // 2), jnp.uint32),
                   jax.ShapeDtypeStruct((TOP_K, t_moe), jnp.int32), jax.ShapeDtypeStruct((TOP_K, t_moe), F32),
                   jax.ShapeDtypeStruct((n_moe, n_exp, 1), F32)],
        compiler_params=_params("parallel"), name="merge_router",
    )(ya_p, yr_p, ga, gb, x_p, ya_s, yr_s, ga_x, gb_x, x_x, wa, wb, wo, nf, wr_t, br)

    BM = TM
    cnt = cnt[:, :, 0].astype(jnp.int32)
    cnt = (cnt + SEG_ALIGN - 1) // SEG_ALIGN * SEG_ALIGN
    padded = (jnp.sum(cnt, axis=0) + BM - 1) // BM * BM
    pad_end = jnp.cumsum(padded)
    seg_row = (pad_end - padded)[None, :] + jnp.cumsum(cnt, axis=0) - cnt
    seg_loc = jnp.cumsum(cnt, axis=1) - cnt
    n_blk = -(-(t_moe * TOP_K + (SEG_ALIGN - 1) * n_moe * n_exp) // BM) + n_exp
    block_e = jnp.minimum(jnp.sum(pad_end[None, :] <= (jnp.arange(n_blk) * BM)[:, None], axis=1), n_exp - 1).astype(jnp.int32)
    n_used = (pad_end[-1:] // BM).astype(jnp.int32)
    seg_tabs = (cnt.reshape(-1), seg_loc.reshape(-1).astype(jnp.int32), seg_row.reshape(-1).astype(jnp.int32))

    XW = D // 2 + LANES
    n_loc = -(-(TOP_K * TM + SEG_ALIGN * n_exp) // LANES) * LANES
    xs, pos = pl.pallas_call(
        functools.partial(_dispatch_body, n_exp=n_exp),
        grid_spec=pltpu.PrefetchScalarGridSpec(
            num_scalar_prefetch=3, grid=(n_moe,),
            in_specs=[pl.BlockSpec((TOP_K, TM), lambda i, *_: (0, i)), pl.BlockSpec((TOP_K, TM), lambda i, *_: (0, i)),
                      pl.BlockSpec((1, n_exp, 1), lambda i, *_: (i, 0, 0)),
                      pl.BlockSpec((TM, D // 2), lambda i, *_: (i, 0))],
            out_specs=[pl.BlockSpec(memory_space=pl.ANY), pl.BlockSpec((TOP_K, TM), lambda i, *_: (0, i))],
            scratch_shapes=[pltpu.VMEM((n_loc, XW), jnp.uint32), pltpu.VMEM((BM, XW), jnp.uint32),
                            pltpu.SemaphoreType.DMA]),
        out_shape=[jax.ShapeDtypeStruct((n_blk * BM, XW), jnp.uint32), jax.ShapeDtypeStruct((TOP_K, t_moe), jnp.int32)],
        compiler_params=_params("arbitrary"), name="moe_dispatch",
    )(*seg_tabs, ids, gates, seg_loc.astype(F32)[:, :, None], tok)

    blk = lambda i, be, nu: jnp.minimum(i, nu[0] - 1)
    exp_w = lambda a: pl.BlockSpec((1,) + a.shape[1:], lambda i, be, nu: (be[blk(i, be, nu)],) + (0,) * (a.ndim - 1))
    wgu, bgu = w_gate_up[0].astype(BF16), b_gate_up[0][:, None, :]
    wd, bd = w_down[0].astype(BF16), b_down[0][:, None, :]
    os_ = pl.pallas_call(
        functools.partial(_expert_body, d_ff=d_ff),
        grid_spec=pltpu.PrefetchScalarGridSpec(
            num_scalar_prefetch=2, grid=(n_blk,),
            in_specs=[pl.BlockSpec((BM, XW), lambda i, be, nu: (blk(i, be, nu), 0)),
                      exp_w(wgu), exp_w(bgu), exp_w(wd), exp_w(bd)],
            out_specs=pl.BlockSpec((BM, D), lambda i, be, nu: (i, 0))),
        out_shape=jax.ShapeDtypeStruct((n_blk * BM, D), jnp.uint32),
        compiler_params=_params("arbitrary"), name="moe_experts",
    )(block_e, n_used, xs, wgu, bgu, wd, bd)

    def combine(off, n):
        return pl.pallas_call(
            functools.partial(_combine_body, n_exp=n_exp, blk_off=off),
            grid_spec=pltpu.PrefetchScalarGridSpec(
                num_scalar_prefetch=3, grid=(n,),
                in_specs=[pl.BlockSpec((TOP_K, TM), lambda i, *_: (0, i + off)),
                          pl.BlockSpec((TM, D), lambda i, *_: (i + off, 0)),
                          pl.BlockSpec((1, D), lambda i, *_: (0, 0)), pl.BlockSpec(memory_space=pl.ANY)],
                out_specs=pl.BlockSpec((TM, D), lambda i, *_: (i, 0)),
                scratch_shapes=[pltpu.VMEM((n_loc, D), jnp.uint32), pltpu.SemaphoreType.DMA]),
            out_shape=jax.ShapeDtypeStruct((n * TM, D), F32),
            compiler_params=_params("arbitrary"), name="moe_combine",
        )(*seg_tabs, pos, h1, norm_final[None, :], os_)

    y_prompt = combine(0, n_p).reshape(B, SEQ, D)
    y_sample = combine(n_p, n_moe - n_p).reshape(Bs, Ts, D)

    def with_meta(a, a_x):
        meta_rows = jnp.broadcast_to(a_x[t_s:][None], (B, n_meta, sbw))
        return jnp.concatenate([meta_rows, a.reshape(B, SEQ, sbw)], axis=1).reshape(1, B, n_meta + SEQ, h_sb, dh)

    return (y_prompt, y_sample, with_meta(ka, ka_x), with_meta(va, va_x), s_p[None],
            ka_x[:t_s].reshape(1, Bs, Ts, h_sb, dh), va_x[:t_s].reshape(1, Bs, Ts, h_sb, dh), s_s[None])
```

```python
import functools

import jax
import jax.numpy as jnp
from jax import lax
from jax.experimental import pallas as pl
from jax.experimental.pallas import tpu as pltpu

F32 = jnp.float32
BF16 = jnp.bfloat16

EPS = 1e-6
ROPE_BASE = 10000.0
TOP_K = 4
SWIGLU_LIMIT = 7.0
SWIGLU_ALPHA = 1.702
LOG2E = 1.4426950408889634

LANES = 128
SEG_ALIGN = 8
VMEM_LIMIT = 56 << 20

_NT = (((1,), (1,)), ((), ()))
_TN = (((0,), (0,)), ((), ()))


def _params(*sem, **kw):
    return pltpu.CompilerParams(dimension_semantics=sem, vmem_limit_bytes=VMEM_LIMIT, **kw)


def _dot(a, b):
    return jnp.dot(a, b, preferred_element_type=F32)


def _rms(x, g):
    return x * lax.rsqrt(jnp.mean(x * x, axis=-1, keepdims=True) + EPS) * g


def _inproj_body(x_ref, g_ref, w_ref, cs_ref, sn_ref, qa_ref, ka_ref, va_ref, qr_ref, kr_ref,
                 vr_ref, gr_ref, ga_ref, gb_ref, *, offs, q_scale, k_scale, n_ret, dk):
    xn = _rms(x_ref[...], g_ref[...]).astype(BF16)

    def proj(i):
        return _dot(xn, w_ref[:, offs[i]:offs[i + 1]])

    qa_ref[...] = (proj(0) * q_scale).astype(BF16)
    ka_ref[...] = proj(1)
    va_ref[...] = proj(2)
    cs = cs_ref[...]
    sn = sn_ref[...]
    for ref, i, scale in ((qr_ref, 3, None), (kr_ref, 4, k_scale)):
        t = proj(i)
        for h in range(n_ret):
            th = t[:, h * dk:(h + 1) * dk]
            r = th * cs + pltpu.roll(th, dk // 2, axis=1) * sn
            ref[:, h * dk:(h + 1) * dk] = r if scale is None else r * scale
    vr_ref[...] = proj(5).astype(BF16)
    gr_ref[...] = proj(6)
    ga_ref[...] = proj(7)
    gb_ref[...] = proj(8)


def _sb_scores(z):
    log_beta = jnp.minimum(z, 0.0) - jnp.log(1.0 + jnp.exp(-jnp.abs(z)))
    return log_beta, log_beta - z


def _split_bf16(x):
    hi = x.astype(BF16)
    return hi, (x - hi.astype(F32)).astype(BF16)


def _suffix_ones(n):
    r = lax.broadcasted_iota(jnp.int32, (2 * n, n), 0) & (n - 1)
    c = lax.broadcasted_iota(jnp.int32, (2 * n, n), 1)
    return jnp.where(r >= c, 1.0, 0.0).astype(BF16)


def _sb_block(qs, kblk, vblk, suffix2, mask, carry, acc):
    z2 = lax.dot_general(qs, kblk, _NT, preferred_element_type=F32) * LOG2E
    neg_abs = lax.bitcast_convert_type(lax.bitcast_convert_type(z2, jnp.uint32) | jnp.uint32(0x80000000), F32)
    drop = jnp.maximum(z2, 0.0) + jnp.log2(1.0 + jnp.exp2(neg_abs))
    if mask is not None:
        drop = jnp.where(mask, drop, 0.0)
    hi, lo = _split_bf16(drop)
    incl = _dot(jnp.concatenate([hi, lo], axis=1), suffix2)
    w = jnp.exp2(z2 - (incl + carry))
    if mask is not None:
        w = jnp.where(mask, w, 0.0)
    return carry + incl[:, 0:1], acc + _dot(w.astype(BF16), vblk)


def _sb_prompt_body(q_ref, k_ref, v_ref, mk_ref, mv_ref, o_ref, kb, vb, carry_ref, acc_ref, *, tq, n_meta, dh):
    qi = pl.program_id(2)
    tk = tq // 2
    mpad = LANES - n_meta
    n_heads = LANES // dh

    @pl.when(qi == 0)
    def _():
        kb[0:mpad, :] = jnp.zeros((mpad, LANES), BF16)
        vb[0:mpad, :] = jnp.zeros((mpad, LANES), BF16)
        kb[mpad:LANES, :] = mk_ref[...].astype(BF16)
        vb[mpad:LANES, :] = mv_ref[...].astype(BF16)
        kb[LANES:, :] = k_ref[...].astype(BF16)
        vb[LANES:, :] = v_ref[...].astype(BF16)

    q = q_ref[...]
    lane = lax.broadcasted_iota(jnp.int32, (tk, LANES), 1)
    parts = []
    for half in range(2):
        qh = q[half * tk:(half + 1) * tk]
        for head in range(n_heads):
            parts.append(jnp.where(lane // dh == head, qh, jnp.zeros_like(qh)))
    qs = jnp.concatenate(parts, axis=0)
    m_half = n_heads * tk
    suffix2 = _suffix_ones(tk)
    base = pl.multiple_of(LANES + qi * tq, LANES)

    r = lax.broadcasted_iota(jnp.int32, (m_half, tk), 0) & (tk - 1)
    c = lax.broadcasted_iota(jnp.int32, (m_half, tk), 1)
    carry_b, acc_b = _sb_block(qs[m_half:], kb[pl.ds(base + tk, tk), :], vb[pl.ds(base + tk, tk), :], suffix2,
                               c < r, jnp.zeros((m_half, 1), F32), jnp.zeros((m_half, LANES), F32))
    carry = jnp.concatenate([jnp.zeros((m_half, 1), F32), carry_b], axis=0)
    acc = jnp.concatenate([jnp.zeros((m_half, LANES), F32), acc_b], axis=0)
    r = lax.broadcasted_iota(jnp.int32, (2 * m_half, tk), 0)
    c = lax.broadcasted_iota(jnp.int32, (2 * m_half, tk), 1)
    carry, acc = _sb_block(qs, kb[pl.ds(base, tk), :], vb[pl.ds(base, tk), :], suffix2,
                           (r >= m_half) | (c < (r & (tk - 1))), carry, acc)
    carry_ref[...] = carry
    acc_ref[...] = acc

    def left(it, _):
        carry, acc = carry_ref[...], acc_ref[...]
        for step in (1, 2):
            off = pl.multiple_of(base - (2 * it + step) * tk, LANES)
            carry, acc = _sb_block(qs, kb[pl.ds(off, tk), :], vb[pl.ds(off, tk), :], suffix2, None, carry, acc)
        carry_ref[...] = carry
        acc_ref[...] = acc
        return 0

    lax.fori_loop(0, qi, left, 0)
    c = lax.broadcasted_iota(jnp.int32, (2 * m_half, LANES), 1)
    _, acc = _sb_block(qs, kb[0:LANES, :], vb[0:LANES, :], _suffix_ones(LANES), c >= mpad,
                       carry_ref[...], acc_ref[...])
    for half in range(2):
        for head in range(n_heads):
            rows = (half * n_heads + head) * tk
            o_ref[half * tk:(half + 1) * tk, head * dh:(head + 1) * dh] = (
                acc[rows:rows + tk, head * dh:(head + 1) * dh].astype(o_ref.dtype))


def _sb_sample_body(q_ref, nk_ref, nv_ref, mk_ref, mv_ref, ck_ref, cv_ref, o_ref,
                    qbd, carry_ref, acc_ref, *, ts, n_meta, dh, sub):
    j = pl.program_id(1)
    width = q_ref.shape[1]
    n_heads = width // dh

    def head_mask(shape):
        r = lax.broadcasted_iota(jnp.int32, shape, 0)
        c = lax.broadcasted_iota(jnp.int32, shape, 1)
        return r // ts == c // dh

    def sweep(k, v, mask, carry, acc):
        n = k.shape[0]
        z = lax.dot_general(k.astype(BF16), qbd[...], _NT, preferred_element_type=F32)
        log_beta, log_keep = _sb_scores(z)
        if mask is not None:
            log_keep = jnp.where(mask, log_keep, 0.0)
        r = lax.broadcasted_iota(jnp.int32, (n, n), 0)
        c = lax.broadcasted_iota(jnp.int32, (n, n), 1)
        suffix = jnp.where(c >= r, 1.0, 0.0).astype(BF16)
        hi, lo = _split_bf16(log_keep)
        incl = _dot(suffix, hi) + _dot(suffix, lo)
        w = jnp.exp(log_beta + (carry + incl - log_keep))
        if mask is not None:
            w = jnp.where(mask, w, 0.0)
        acc = acc + lax.dot_general(w.astype(BF16), v.astype(BF16), _TN, preferred_element_type=F32)
        return carry + incl[0:1, :], acc

    @pl.when(j == 0)
    def _():
        qt = jnp.concatenate([q_ref[...]] * n_heads, axis=0)
        qbd[...] = jnp.where(head_mask((n_heads * ts, width)), qt, jnp.zeros_like(qt))
        key = lax.broadcasted_iota(jnp.int32, (ts, LANES), 0)
        qry = lax.broadcasted_iota(jnp.int32, (ts, LANES), 1) % ts
        carry, acc = sweep(nk_ref[...], nv_ref[...], key < qry,
                           jnp.zeros((1, LANES), F32), jnp.zeros((LANES, width), F32))
        carry_ref[...] = carry
        acc_ref[...] = acc

    carry = carry_ref[...]
    acc = acc_ref[...]
    chunk = ck_ref.shape[1]
    for s in range(chunk // sub - 1, -1, -1):
        carry, acc = sweep(ck_ref[0, s * sub:(s + 1) * sub, :], cv_ref[0, s * sub:(s + 1) * sub, :],
                           None, carry, acc)
    carry_ref[...] = carry
    acc_ref[...] = acc

    @pl.when(j == pl.num_programs(1) - 1)
    def _():
        _, acc = sweep(mk_ref[...], mv_ref[...], None, carry_ref[...], acc_ref[...])
        own = jnp.where(head_mask((n_heads * ts, width)), acc, 0.0)
        o_ref[...] = jnp.sum(own.reshape(n_heads, ts, width), axis=0).astype(o_ref.dtype)


def _ret_chunk(q, k, v, state, dec, rowdec, coldec, gc):
    qb = q.astype(BF16)
    inner = lax.dot_general(qb, k.astype(BF16), _NT, preferred_element_type=F32) * dec
    o = _dot(inner.astype(BF16), v) + _dot(qb, state.astype(BF16)) * rowdec
    kw = (k * coldec).astype(BF16)
    return gc * state + lax.dot_general(kw, v, _TN, preferred_element_type=F32), o


def _ret_gate(o, g):
    on = o * lax.rsqrt(jnp.mean(o * o, axis=-1, keepdims=True) + EPS)
    return (g * jax.nn.sigmoid(g)) * on


def _ret_prompt_body(q_ref, k_ref, v_ref, g_ref, mk_ref, mv_ref, dec_ref, rd_ref, cd_ref, gc_ref,
                     mcd_ref, y_ref, s_ref, *, chunk):
    kw = (mk_ref[...] * mcd_ref[0]).astype(BF16)
    state = lax.dot_general(kw, mv_ref[...], _TN, preferred_element_type=F32)
    dec, rd, cd, gc = dec_ref[0], rd_ref[0], cd_ref[0], gc_ref[0]

    def step(c, state):
        r = pl.ds(pl.multiple_of(c * chunk, chunk), chunk)
        state, o = _ret_chunk(q_ref[r, :], k_ref[r, :], v_ref[r, :], state, dec, rd, cd, gc)
        y_ref[r, :] = _ret_gate(o, g_ref[r, :]).astype(y_ref.dtype)
        return state

    s_ref[0, 0] = lax.fori_loop(0, q_ref.shape[0] // chunk, step, state)


def _ret_sample_body(q_ref, k_ref, v_ref, g_ref, s0_ref, dec_ref, rd_ref, cd_ref, gc_ref,
                     y_ref, s_ref):
    state, o = _ret_chunk(q_ref[...], k_ref[...], v_ref[...], s0_ref[0, 0], dec_ref[0], rd_ref[0],
                          cd_ref[0], gc_ref[0])
    y_ref[...] = _ret_gate(o, g_ref[...]).astype(y_ref.dtype)
    s_ref[0, 0] = state


def _pack_bf16_pairs(x):
    bits = lax.bitcast_convert_type(x.astype(BF16).astype(F32), jnp.uint32)
    w = x.shape[1] // 2
    return (bits[:, :w] >> 16) | bits[:, w:]


def _unpack_bf16_pairs(p):
    lo = lax.bitcast_convert_type(p << 16, F32).astype(BF16)
    hi = lax.bitcast_convert_type(p & jnp.uint32(0xFFFF0000), F32).astype(BF16)
    return lo, hi


def _merge_body(*refs, n_prompt, top_k):
    prompt, sample, rest = refs[0:5], refs[5:10], refs[10:]
    i = pl.program_id(0)

    @pl.when(i < n_prompt)
    def _():
        _merge_block(*prompt, *rest, top_k=top_k)

    @pl.when(i >= n_prompt)
    def _():
        _merge_block(*sample, *rest, top_k=top_k)


def _merge_block(ya_ref, yr_ref, ga_ref, gb_ref, x_ref, wa_ref, wb_ref, wo_ref, nf_ref, wr_ref, br_ref,
                 h_ref, tok_ref, ids_ref, gt_ref, cnt_ref, *, top_k):
    mix = (jax.nn.sigmoid(ga_ref[...]) * _dot(ya_ref[...], wa_ref[...])
           + jax.nn.sigmoid(gb_ref[...]) * _dot(yr_ref[...], wb_ref[...]))
    h = x_ref[...] + _dot(mix.astype(BF16), wo_ref[...])
    h_ref[...] = h
    tok = _rms(h, nf_ref[...])
    tok_ref[...] = _pack_bf16_pairs(tok)
    logits = lax.dot_general(wr_ref[...], tok.astype(BF16), _NT, preferred_element_type=F32) + br_ref[...]
    n_exp, tm = logits.shape
    eidx = lax.broadcasted_iota(jnp.int32, (n_exp, tm), 0).astype(F32)
    ids, tops = [], []
    for _ in range(top_k):
        m = jnp.max(logits, axis=0, keepdims=True)
        sel = jnp.min(jnp.where(logits == m, eidx, float(n_exp)), axis=0, keepdims=True)
        ids.append(sel)
        tops.append(m)
        logits = jnp.where(eidx == sel, -jnp.inf, logits)
    ex = [jnp.exp(t - tops[0]) for t in tops]
    inv = 1.0 / functools.reduce(lambda a, b: a + b, ex)
    ids_ref[...] = jnp.concatenate(ids, axis=0).astype(jnp.int32)
    gt_ref[...] = jnp.concatenate([e * inv for e in ex], axis=0)
    cnt = jnp.zeros((n_exp, 1), F32)
    for sel in ids:
        cnt = cnt + jnp.sum(jnp.where(eidx == sel, 1.0, 0.0), axis=1, keepdims=True)
    cnt_ref[0] = cnt


def _local_positions(ids_ref, seg_col, n_exp):
    top_k, tb = ids_ref.shape
    ids = ids_ref[...].astype(F32)
    eidx = lax.broadcasted_iota(jnp.int32, (n_exp, tb), 0).astype(F32)
    r = lax.broadcasted_iota(jnp.int32, (tb, tb), 0)
    c = lax.broadcasted_iota(jnp.int32, (tb, tb), 1)
    before = jnp.where(r < c, 1.0, 0.0).astype(BF16)
    run = seg_col
    rows = []
    for k in range(top_k):
        onehot = jnp.where(ids[k:k + 1, :] == eidx, 1.0, 0.0)
        pos = _dot(onehot.astype(BF16), before) + run
        rows.append(jnp.sum(onehot * pos, axis=0, keepdims=True))
        run = run + jnp.sum(onehot, axis=1, keepdims=True)
    return jnp.concatenate(rows, axis=0)


def _segment_copies(cnt_ref, seg_ref, row_ref, blk, n_exp, max_rows, vmem, hbm, sem, *, to_hbm, wait):
    def per_expert(e, _):
        n = cnt_ref[blk * n_exp + e]
        src0 = seg_ref[blk * n_exp + e]
        dst0 = row_ref[blk * n_exp + e]
        off = jnp.int32(0)
        for bit in range(max_rows.bit_length() - 1, SEG_ALIGN.bit_length() - 2, -1):
            size = 1 << bit
            take = (n >> bit) & 1

            @pl.when(take == 1)
            def _(off=off, size=size):
                v = vmem.at[pl.ds(pl.multiple_of(src0 + off, SEG_ALIGN), size), :]
                h = hbm.at[pl.ds(pl.multiple_of(dst0 + off, SEG_ALIGN), size), :]
                cp = pltpu.make_async_copy(v, h, sem) if to_hbm else pltpu.make_async_copy(h, v, sem)
                if wait:
                    cp.wait()
                else:
                    cp.start()

            off = off + take * size
        return 0

    lax.fori_loop(0, n_exp, per_expert, 0)


def _permutation(pos, n_rows):
    p = lax.broadcasted_iota(jnp.int32, (n_rows, pos.shape[1]), 0)
    out = jnp.where(pos[0:1, :] == p, 1.0, 0.0)
    for k in range(1, pos.shape[0]):
        out = out + jnp.where(pos[k:k + 1, :] == p, 1.0, 0.0)
    return out


def _dispatch_body(cnt_ref, seg_ref, row_ref, ids_ref, gt_ref, segc_ref, tok_ref, xs_out, pos_ref,
                   loc, zeros, sems, *, n_exp):
    blk = pl.program_id(0)
    last = pl.num_programs(0) - 1
    slot = blk % 2
    top_k, tb = ids_ref.shape
    zb = zeros.shape[0]
    words = tok_ref.shape[1]
    n_rows = loc.shape[1]

    def copies(b, s, wait):
        _segment_copies(cnt_ref, seg_ref, row_ref, b, n_exp, n_rows, loc.at[s], xs_out, sems.at[s],
                        to_hbm=True, wait=wait)

    @pl.when(blk == 0)
    def _():
        zeros[...] = jnp.zeros_like(zeros)

        def fill(b, _):
            pltpu.make_async_copy(zeros, xs_out.at[pl.ds(pl.multiple_of(b * zb, zb), zb), :], sems.at[0]).start()
            return 0

        def done(b, _):
            pltpu.make_async_copy(zeros, xs_out.at[pl.ds(pl.multiple_of(b * zb, zb), zb), :], sems.at[0]).wait()
            return 0

        lax.fori_loop(0, xs_out.shape[0] // zb, fill, 0)
        lax.fori_loop(0, xs_out.shape[0] // zb, done, 0)

    @pl.when(blk >= 2)
    def _():
        copies(blk - 2, slot, True)

    pos = _local_positions(ids_ref, segc_ref[0], n_exp).astype(jnp.int32)
    pos_ref[...] = pos
    perm = _permutation(pos, n_rows).astype(BF16)
    lo, hi = _unpack_bf16_pairs(tok_ref[...])
    lo_bits = lax.bitcast_convert_type(_dot(perm, lo), jnp.uint32)
    hi_bits = lax.bitcast_convert_type(_dot(perm, hi), jnp.uint32)
    buf = loc.at[slot]
    buf[:, 0:words] = (lo_bits >> 16) | hi_bits
    p = lax.broadcasted_iota(jnp.int32, (n_rows, tb), 0)
    gates = gt_ref[...]
    spread = jnp.where(pos[0:1, :] == p, gates[0:1, :], 0.0)
    for k in range(1, top_k):
        spread = spread + jnp.where(pos[k:k + 1, :] == p, gates[k:k + 1, :], 0.0)
    gate_row = jnp.sum(spread, axis=1, keepdims=True)
    buf[:, words:] = lax.bitcast_convert_type(jnp.broadcast_to(gate_row, (n_rows, LANES)), jnp.uint32)
    copies(blk, slot, False)

    @pl.when(blk == last)
    def _():
        @pl.when(blk >= 1)
        def _():
            copies(blk - 1, 1 - slot, True)

        copies(blk, slot, True)


def _expert_body(be_ref, nu_ref, x_ref, wgu_ref, bgu_ref, wd_ref, bd_ref, o_ref, wgu_b, wd_b, *, d_ff):
    i = pl.program_id(0)
    used = i < nu_ref[0]

    @pl.when(jnp.logical_and(used, jnp.logical_or(i == 0, be_ref[i] != be_ref[jnp.maximum(i - 1, 0)])))
    def _():
        wgu_b[...] = wgu_ref[0].astype(BF16)
        wd_b[...] = wd_ref[0].astype(BF16)

    @pl.when(used)
    def _():
        x = x_ref[...]
        half = x.shape[1] - LANES
        lo, hi = _unpack_bf16_pairs(x[:, :half])
        gate = lax.bitcast_convert_type(x[:, half:half + 1], F32)
        h = _dot(lo, wgu_b[:half, :]) + _dot(hi, wgu_b[half:, :]) + bgu_ref[0]
        g = jnp.minimum(h[:, :d_ff], SWIGLU_LIMIT)
        u = jnp.clip(h[:, d_ff:], -SWIGLU_LIMIT, SWIGLU_LIMIT)
        act = (u + 1.0) * (g * jax.nn.sigmoid(SWIGLU_ALPHA * g))
        out = (_dot(act.astype(BF16), wd_b[...]) + bd_ref[0]) * gate
        hi_part, lo_part = _split_bf16(out)
        o_ref[...] = (lax.bitcast_convert_type(hi_part.astype(F32), jnp.uint32)
                      | (lax.bitcast_convert_type(lo_part.astype(F32), jnp.uint32) >> 16))

    @pl.when(jnp.logical_not(used))
    def _():
        o_ref[...] = jnp.zeros_like(o_ref)


def _combine_body(cnt_ref, seg_ref, row_ref, pos_ref, h_ref, nf_ref, os_hbm, y_ref, loc, sems, *, n_exp, blk_off):
    i = pl.program_id(0)
    slot = i % 2
    n_rows = loc.shape[1]
    n_pairs = pos_ref.shape[0] * pos_ref.shape[1]

    def fetch(step, s, wait):
        _segment_copies(cnt_ref, seg_ref, row_ref, step + blk_off, n_exp, n_rows, loc.at[s], os_hbm, sems.at[s],
                        to_hbm=False, wait=wait)

    def prepare(step, s):
        loc[s, n_pairs:, :] = jnp.zeros((n_rows - n_pairs, loc.shape[2]), loc.dtype)
        fetch(step, s, False)

    @pl.when(i == 0)
    def _():
        prepare(0, 0)

    @pl.when(i + 1 < pl.num_programs(0))
    def _():
        prepare(i + 1, 1 - slot)

    perm = _permutation(pos_ref[...], n_rows).astype(BF16)
    fetch(i, slot, True)
    w = loc[slot]
    hi_part = lax.bitcast_convert_type(w & jnp.uint32(0xFFFF0000), F32).astype(BF16)
    lo_part = lax.bitcast_convert_type(w << 16, F32).astype(BF16)
    f = (lax.dot_general(perm, hi_part, _TN, preferred_element_type=F32)
         + lax.dot_general(perm, lo_part, _TN, preferred_element_type=F32))
    y_ref[...] = _rms(h_ref[...] + f, nf_ref[...])


def _largest_pow2_divisor(n, cap):
    t = cap
    while n % t:
        t //= 2
    return t


def _rotation_tables(pos, half):
    inv_freq = 1.0 / (ROPE_BASE ** jnp.linspace(0.0, 1.0, half, dtype=F32))
    ang = pos.astype(F32)[:, None] * inv_freq[None, :]
    cos, sin = jnp.cos(ang), jnp.sin(ang)
    return jnp.concatenate([cos, cos], axis=1), jnp.concatenate([-sin, sin], axis=1)


def _decay_tables(log_g, c):
    idx = jnp.arange(c, dtype=F32)
    lg = log_g[:, None]
    rel = idx[:, None] - idx[None, :]
    dec = jnp.where(rel >= 0, jnp.exp(lg[:, :, None] * jnp.maximum(rel, 0.0)), 0.0)
    rowdec = jnp.exp(lg * (idx + 1.0))[:, :, None]
    coldec = jnp.exp(lg * (c - 1.0 - idx))[:, :, None]
    gc = jnp.exp(lg * c)[:, :, None]
    return dec, rowdec, coldec, gc


def kernel(x_prompt, x_sample, cache_sb_k, cache_sb_v, state_ret, meta_tokens, norm_mix, w_in, w_branch_sb, w_branch_ret, w_out, norm_ffn, w_router, b_router, w_gate_up, b_gate_up, w_down, b_down, norm_final):
    B, SEQ, D = x_prompt.shape
    Bs, Ts, _ = x_sample.shape
    depth, _, past, h_sb, dh = cache_sb_k.shape
    _, _, h_ret, dk, dv = state_ret.shape
    n_meta = meta_tokens.shape[0]
    n_exp = w_router.shape[-1]
    d_ff = w_down.shape[-2]
    sbw, rqk, rv = h_sb * dh, h_ret * dk, h_ret * dv
    assert depth == 1, "one layer: the meta-token rows feed nothing after the mixers"
    assert dk == LANES and 2 * dh == LANES and h_sb * Ts == LANES and n_meta % 16 == 0

    TQ = _largest_pow2_divisor(SEQ, 512)
    assert TQ >= 2 * LANES and n_meta <= LANES
    t_p, t_s = B * SEQ, Bs * Ts
    t_moe = t_p + t_s
    TM = _largest_pow2_divisor(t_moe, 512)
    assert TM >= 16 and SEQ % TM == 0 and t_p % TM == 0 and t_moe % n_meta == 0 and t_p % Ts == 0
    n_p, n_moe = t_p // TM, t_moe // TM
    t_x = t_s + n_meta
    meta_blk = t_s // n_meta

    x_p = x_prompt.reshape(t_p, D)
    x_x = jnp.concatenate([x_sample.reshape(t_s, D), meta_tokens.astype(F32)], axis=0)

    sizes = (sbw,) * 3 + (rqk,) * 2 + (rv,) * 2 + (D,) * 2
    offs = tuple(sum(sizes[:i]) for i in range(len(sizes) + 1))
    n_sq = SEQ // TM
    row = lambda w: pl.BlockSpec((TM, w), lambda i: (i, 0))
    out_dt = (BF16, F32, F32, F32, F32, BF16, F32, F32, F32)
    w_in_b, g_mix = w_in[0].astype(BF16), norm_mix[0][None, :]

    def inproj(x, pos, tm, tab_period):
        rows = x.shape[0]
        blk = lambda w: pl.BlockSpec((tm, w), lambda i: (i, 0))
        tab = pl.BlockSpec((tm, dk), lambda i: (i % tab_period, 0))
        return pl.pallas_call(
            functools.partial(_inproj_body, offs=offs, q_scale=dh ** -0.5, k_scale=dk ** -0.5, n_ret=h_ret, dk=dk),
            grid=(rows // tm,),
            in_specs=[blk(D), pl.BlockSpec((1, D), lambda i: (0, 0)),
                      pl.BlockSpec((D, offs[-1]), lambda i: (0, 0), pipeline_mode=pl.Buffered(1)), tab, tab],
            out_specs=[blk(w) for w in sizes],
            out_shape=[jax.ShapeDtypeStruct((rows, w), dt) for w, dt in zip(sizes, out_dt)],
            compiler_params=_params("parallel"), name="inproj",
        )(x, g_mix, w_in_b, *_rotation_tables(pos, dk // 2))

    qa, ka, va, qr, kr, vr, gr, ga, gb = inproj(x_p, jnp.arange(SEQ), TM, n_sq)
    x_pos = jnp.concatenate([jnp.tile(past + jnp.arange(Ts), Bs), jnp.arange(n_meta) - n_meta])
    qa_x, ka_x, va_x, qr_x, kr_x, vr_x, gr_x, ga_x, gb_x = inproj(x_x, x_pos, t_x, 1)

    n_q = SEQ // TQ
    n_pair = sbw // LANES
    ya_p = pl.pallas_call(
        functools.partial(_sb_prompt_body, tq=TQ, n_meta=n_meta, dh=dh),
        grid=(B, n_pair, n_q),
        in_specs=[pl.BlockSpec((TQ, LANES), lambda b, p, q: (b * n_q + q, p)),
                  pl.BlockSpec((SEQ, LANES), lambda b, p, q: (b, p)),
                  pl.BlockSpec((SEQ, LANES), lambda b, p, q: (b, p)),
                  pl.BlockSpec((n_meta, LANES), lambda b, p, q: (meta_blk, p)),
                  pl.BlockSpec((n_meta, LANES), lambda b, p, q: (meta_blk, p))],
        out_specs=pl.BlockSpec((TQ, LANES), lambda b, p, q: (b * n_q + q, p)),
        out_shape=jax.ShapeDtypeStruct((t_p, sbw), BF16),
        scratch_shapes=[pltpu.VMEM((LANES + SEQ, LANES), BF16), pltpu.VMEM((LANES + SEQ, LANES), BF16),
                        pltpu.VMEM((2 * TQ, 1), F32), pltpu.VMEM((2 * TQ, LANES), F32)],
        compiler_params=_params("parallel", "parallel", "arbitrary"), name="sb_prompt",
    )(qa, ka, va, ka_x, va_x)

    KC = _largest_pow2_divisor(past, 512)
    SUB = min(KC, 256)
    n_kc = past // KC
    smp = lambda w: pl.BlockSpec((Ts, w), lambda b, j: (b, 0))
    met = lambda w: pl.BlockSpec((n_meta, w), lambda b, j: (meta_blk, 0))
    cache = pl.BlockSpec((1, KC, sbw), lambda b, j: (b, n_kc - 1 - j, 0))
    ya_s = pl.pallas_call(
        functools.partial(_sb_sample_body, ts=Ts, n_meta=n_meta, dh=dh, sub=SUB),
        grid=(Bs, n_kc),
        in_specs=[smp(sbw), smp(sbw), smp(sbw), met(sbw), met(sbw), cache, cache],
        out_specs=pl.BlockSpec((Ts, sbw), lambda b, j: (b, 0)),
        out_shape=jax.ShapeDtypeStruct((t_s, sbw), BF16),
        scratch_shapes=[pltpu.VMEM((LANES, sbw), BF16), pltpu.VMEM((1, LANES), F32), pltpu.VMEM((LANES, sbw), F32)],
        compiler_params=_params("parallel", "arbitrary"), name="sb_sample",
    )(qa_x, ka_x, va_x, ka_x, va_x, cache_sb_k[0].reshape(Bs, past, sbw), cache_sb_v[0].reshape(Bs, past, sbw))

    log_g = jnp.log(1.0 - 2.0 ** (-5.0 - jnp.arange(h_ret, dtype=F32)))
    RC = _largest_pow2_divisor(SEQ, 256)
    dec, rd, cd, gc = _decay_tables(log_g, RC)
    _, _, mcd, _ = _decay_tables(log_g, n_meta)
    per_b = lambda w: pl.BlockSpec((SEQ, w), lambda b, h: (b, h))
    met_h = lambda w: pl.BlockSpec((n_meta, w), lambda b, h: (meta_blk, h))
    tab = lambda a: pl.BlockSpec((1,) + a.shape[1:], lambda b, h: (h, 0, 0))
    state_spec = pl.BlockSpec((1, 1, dk, dv), lambda b, h: (b, h, 0, 0))
    yr_p, s_p = pl.pallas_call(
        functools.partial(_ret_prompt_body, chunk=RC),
        grid=(B, h_ret),
        in_specs=[per_b(dk), per_b(dk), per_b(dv), per_b(dv), met_h(dk), met_h(dv),
                  tab(dec), tab(rd), tab(cd), tab(gc), tab(mcd)],
        out_specs=[per_b(dv), state_spec],
        out_shape=[jax.ShapeDtypeStruct((t_p, rv), BF16), jax.ShapeDtypeStruct((B, h_ret, dk, dv), F32)],
        compiler_params=_params("parallel", "parallel"), name="ret_prompt",
    )(qr, kr, vr, gr, kr_x, vr_x, dec, rd, cd, gc, mcd)

    dec_s, rd_s, cd_s, gc_s = _decay_tables(log_g, Ts)
    smp_h = lambda w: pl.BlockSpec((Ts, w), lambda b, h: (b, h))
    yr_s, s_s = pl.pallas_call(
        _ret_sample_body,
        grid=(Bs, h_ret),
        in_specs=[smp_h(dk), smp_h(dk), smp_h(dv), smp_h(dv), state_spec,
                  tab(dec_s), tab(rd_s), tab(cd_s), tab(gc_s)],
        out_specs=[smp_h(dv), state_spec],
        out_shape=[jax.ShapeDtypeStruct((t_s, rv), BF16), jax.ShapeDtypeStruct((Bs, h_ret, dk, dv), F32)],
        compiler_params=_params("parallel", "parallel"), name="ret_sample",
    )(qr_x, kr_x, vr_x, gr_x, state_ret[0], dec_s, rd_s, cd_s, gc_s)

    const = lambda a: pl.BlockSpec(a.shape, lambda i: (0,) * a.ndim)
    prm = lambda w: pl.BlockSpec((TM, w), lambda i: (jnp.minimum(i, n_p - 1), 0))
    smp_m = lambda w: pl.BlockSpec((TM, w), lambda i: (jnp.maximum(i - n_p, 0), 0))
    lane_major = pl.BlockSpec((TOP_K, TM), lambda i: (0, i))
    wa, wb, wo = w_branch_sb[0].astype(BF16), w_branch_ret[0].astype(BF16), w_out[0].astype(BF16)
    nf, wr_t, br = norm_ffn[0][None, :], w_router[0].T.astype(BF16), b_router[0][:, None]
    h1, tok, ids, gates, cnt = pl.pallas_call(
        functools.partial(_merge_body, n_prompt=n_p, top_k=TOP_K),
        grid=(n_moe,),
        in_specs=[prm(sbw), prm(rv), prm(D), prm(D), prm(D), smp_m(sbw), smp_m(rv), smp_m(D), smp_m(D), smp_m(D),
                  const(wa), const(wb), const(wo), const(nf), const(wr_t), const(br)],
        out_specs=[row(D), row(D // 2), lane_major, lane_major, pl.BlockSpec((1, n_exp, 1), lambda i: (i, 0, 0))],
        out_shape=[jax.ShapeDtypeStruct((t_moe, D), F32), jax.ShapeDtypeStruct((t_moe, D // 2), jnp.uint32),
                   jax.ShapeDtypeStruct((TOP_K, t_moe), jnp.int32), jax.ShapeDtypeStruct((TOP_K, t_moe), F32),
                   jax.ShapeDtypeStruct((n_moe, n_exp, 1), F32)],
        compiler_params=_params("parallel"), name="merge_router",
    )(ya_p, yr_p, ga, gb, x_p, ya_s, yr_s, ga_x, gb_x, x_x, wa, wb, wo, nf, wr_t, br)

    BM = TM
    cnt = cnt[:, :, 0].astype(jnp.int32)
    cnt = (cnt + SEG_ALIGN - 1) // SEG_ALIGN * SEG_ALIGN
    padded = (jnp.sum(cnt, axis=0) + BM - 1) // BM * BM
    pad_end = jnp.cumsum(padded)
    seg_row = (pad_end - padded)[None, :] + jnp.cumsum(cnt, axis=0) - cnt
    seg_loc = jnp.cumsum(cnt, axis=1) - cnt
    n_blk = -(-(t_moe * TOP_K + (SEG_ALIGN - 1) * n_moe * n_exp) // BM) + n_exp
    block_e = jnp.minimum(jnp.sum(pad_end[None, :] <= (jnp.arange(n_blk) * BM)[:, None], axis=1), n_exp - 1).astype(jnp.int32)
    n_used = (pad_end[-1:] // BM).astype(jnp.int32)
    seg_tabs = (cnt.reshape(-1), seg_loc.reshape(-1).astype(jnp.int32), seg_row.reshape(-1).astype(jnp.int32))

    XW = D // 2 + LANES
    n_loc = -(-(TOP_K * TM + SEG_ALIGN * n_exp) // LANES) * LANES
    xs, pos = pl.pallas_call(
        functools.partial(_dispatch_body, n_exp=n_exp),
        grid_spec=pltpu.PrefetchScalarGridSpec(
            num_scalar_prefetch=3, grid=(n_moe,),
            in_specs=[pl.BlockSpec((TOP_K, TM), lambda i, *_: (0, i)), pl.BlockSpec((TOP_K, TM), lambda i, *_: (0, i)),
                      pl.BlockSpec((1, n_exp, 1), lambda i, *_: (i, 0, 0)),
                      pl.BlockSpec((TM, D // 2), lambda i, *_: (i, 0))],
            out_specs=[pl.BlockSpec(memory_space=pl.ANY), pl.BlockSpec((TOP_K, TM), lambda i, *_: (0, i))],
            scratch_shapes=[pltpu.VMEM((2, n_loc, XW), jnp.uint32), pltpu.VMEM((BM, XW), jnp.uint32),
                            pltpu.SemaphoreType.DMA((2,))]),
        out_shape=[jax.ShapeDtypeStruct((n_blk * BM, XW), jnp.uint32), jax.ShapeDtypeStruct((TOP_K, t_moe), jnp.int32)],
        compiler_params=_params("arbitrary"), name="moe_dispatch",
    )(*seg_tabs, ids, gates, seg_loc.astype(F32)[:, :, None], tok)

    blk = lambda i, be, nu: jnp.minimum(i, nu[0] - 1)
    exp_w = lambda a: pl.BlockSpec((1,) + a.shape[1:], lambda i, be, nu: (be[blk(i, be, nu)],) + (0,) * (a.ndim - 1))
    wgu, bgu = w_gate_up[0], b_gate_up[0][:, None, :]
    wd, bd = w_down[0], b_down[0][:, None, :]
    os_ = pl.pallas_call(
        functools.partial(_expert_body, d_ff=d_ff),
        grid_spec=pltpu.PrefetchScalarGridSpec(
            num_scalar_prefetch=2, grid=(n_blk,),
            in_specs=[pl.BlockSpec((BM, XW), lambda i, be, nu: (blk(i, be, nu), 0)),
                      exp_w(wgu), exp_w(bgu), exp_w(wd), exp_w(bd)],
            out_specs=pl.BlockSpec((BM, D), lambda i, be, nu: (i, 0)),
            scratch_shapes=[pltpu.VMEM(wgu.shape[1:], BF16), pltpu.VMEM(wd.shape[1:], BF16)]),
        out_shape=jax.ShapeDtypeStruct((n_blk * BM, D), jnp.uint32),
        compiler_params=_params("arbitrary"), name="moe_experts",
    )(block_e, n_used, xs, wgu, bgu, wd, bd)

    def combine(off, n):
        return pl.pallas_call(
            functools.partial(_combine_body, n_exp=n_exp, blk_off=off),
            grid_spec=pltpu.PrefetchScalarGridSpec(
                num_scalar_prefetch=3, grid=(n,),
                in_specs=[pl.BlockSpec((TOP_K, TM), lambda i, *_: (0, i + off)),
                          pl.BlockSpec((TM, D), lambda i, *_: (i + off, 0)),
                          pl.BlockSpec((1, D), lambda i, *_: (0, 0)), pl.BlockSpec(memory_space=pl.ANY)],
                out_specs=pl.BlockSpec((TM, D), lambda i, *_: (i, 0)),
                scratch_shapes=[pltpu.VMEM((2, n_loc, D), jnp.uint32), pltpu.SemaphoreType.DMA((2,))]),
            out_shape=jax.ShapeDtypeStruct((n * TM, D), F32),
            compiler_params=_params("arbitrary"), name="moe_combine",
        )(*seg_tabs, pos, h1, norm_final[None, :], os_)

    y_prompt = combine(0, n_p).reshape(B, SEQ, D)
    y_sample = combine(n_p, n_moe - n_p).reshape(Bs, Ts, D)

    def with_meta(a, a_x):
        meta_rows = jnp.broadcast_to(a_x[t_s:][None], (B, n_meta, sbw))
        return jnp.concatenate([meta_rows, a.reshape(B, SEQ, sbw)], axis=1).reshape(1, B, n_meta + SEQ, h_sb, dh)

    return (y_prompt, y_sample, with_meta(ka, ka_x), with_meta(va, va_x), s_p[None],
            ka_x[:t_s].reshape(1, Bs, Ts, h_sb, dh), va_x[:t_s].reshape(1, Bs, Ts, h_sb, dh), s_s[None])
```

```python
import functools

import jax
import jax.numpy as jnp
from jax import lax
from jax.experimental import pallas as pl
from jax.experimental.pallas import tpu as pltpu

F32 = jnp.float32
BF16 = jnp.bfloat16

EPS = 1e-6
ROPE_BASE = 10000.0
TOP_K = 4
SWIGLU_LIMIT = 7.0
SWIGLU_ALPHA = 1.702
LOG2E = 1.4426950408889634

LANES = 128
SEG_ALIGN = 8
SEG_BIG = 256
VMEM_LIMIT = 56 << 20

_NT = (((1,), (1,)), ((), ()))
_TN = (((0,), (0,)), ((), ()))


def _params(*sem, **kw):
    return pltpu.CompilerParams(dimension_semantics=sem, vmem_limit_bytes=VMEM_LIMIT, **kw)


def _dot(a, b):
    return jnp.dot(a, b, preferred_element_type=F32)


def _rms(x, g):
    return x * lax.rsqrt(jnp.mean(x * x, axis=-1, keepdims=True) + EPS) * g


def _inproj_body(x_ref, g_ref, w_ref, cs_ref, sn_ref, qa_ref, ka_ref, va_ref, qr_ref, kr_ref,
                 vr_ref, gr_ref, ga_ref, gb_ref, *, offs, q_scale, k_scale, n_ret, dk):
    xn = _rms(x_ref[...], g_ref[...]).astype(BF16)

    def proj(i):
        return _dot(xn, w_ref[:, offs[i]:offs[i + 1]])

    qa_ref[...] = (proj(0) * q_scale).astype(BF16)
    ka_ref[...] = proj(1)
    va_ref[...] = proj(2)
    cs = cs_ref[...]
    sn = sn_ref[...]
    for ref, i, scale in ((qr_ref, 3, None), (kr_ref, 4, k_scale)):
        t = proj(i)
        for h in range(n_ret):
            th = t[:, h * dk:(h + 1) * dk]
            r = th * cs + pltpu.roll(th, dk // 2, axis=1) * sn
            ref[:, h * dk:(h + 1) * dk] = r if scale is None else r * scale
    vr_ref[...] = proj(5).astype(BF16)
    gr_ref[...] = proj(6)
    ga_ref[...] = proj(7)
    gb_ref[...] = proj(8)


def _sb_scores(z):
    log_beta = jnp.minimum(z, 0.0) - jnp.log(1.0 + jnp.exp(-jnp.abs(z)))
    return log_beta, log_beta - z


def _split_bf16(x):
    hi = x.astype(BF16)
    return hi, (x - hi.astype(F32)).astype(BF16)


def _suffix_ones(n):
    r = lax.broadcasted_iota(jnp.int32, (2 * n, n), 0) & (n - 1)
    c = lax.broadcasted_iota(jnp.int32, (2 * n, n), 1)
    return jnp.where(r >= c, 1.0, 0.0).astype(BF16)


def _sb_block(qs, kblk, vblk, suffix2, mask, carry, acc):
    z2 = lax.dot_general(qs, kblk, _NT, preferred_element_type=F32) * LOG2E
    neg_abs = lax.bitcast_convert_type(lax.bitcast_convert_type(z2, jnp.uint32) | jnp.uint32(0x80000000), F32)
    drop = jnp.maximum(z2, 0.0) + jnp.log2(1.0 + jnp.exp2(neg_abs))
    if mask is not None:
        drop = jnp.where(mask, drop, 0.0)
    hi, lo = _split_bf16(drop)
    incl = _dot(jnp.concatenate([hi, lo], axis=1), suffix2)
    w = jnp.exp2(z2 - (incl + carry))
    if mask is not None:
        w = jnp.where(mask, w, 0.0)
    return carry + incl[:, 0:1], acc + _dot(w.astype(BF16), vblk)


def _sb_prompt_body(q_ref, k_ref, v_ref, mk_ref, mv_ref, o_ref, kb, vb, carry_ref, acc_ref, *, tq, n_meta, dh):
    qi = pl.program_id(2)
    tk = tq // 2
    mpad = LANES - n_meta
    n_heads = LANES // dh

    @pl.when(qi == 0)
    def _():
        kb[0:mpad, :] = jnp.zeros((mpad, LANES), BF16)
        vb[0:mpad, :] = jnp.zeros((mpad, LANES), BF16)
        kb[mpad:LANES, :] = mk_ref[...].astype(BF16)
        vb[mpad:LANES, :] = mv_ref[...].astype(BF16)
        kb[LANES:, :] = k_ref[...].astype(BF16)
        vb[LANES:, :] = v_ref[...].astype(BF16)

    q = q_ref[...]
    lane = lax.broadcasted_iota(jnp.int32, (tk, LANES), 1)
    parts = []
    for half in range(2):
        qh = q[half * tk:(half + 1) * tk]
        for head in range(n_heads):
            parts.append(jnp.where(lane // dh == head, qh, jnp.zeros_like(qh)))
    qs = jnp.concatenate(parts, axis=0)
    m_half = n_heads * tk
    suffix2 = _suffix_ones(tk)
    base = pl.multiple_of(LANES + qi * tq, LANES)

    r = lax.broadcasted_iota(jnp.int32, (m_half, tk), 0) & (tk - 1)
    c = lax.broadcasted_iota(jnp.int32, (m_half, tk), 1)
    carry_b, acc_b = _sb_block(qs[m_half:], kb[pl.ds(base + tk, tk), :], vb[pl.ds(base + tk, tk), :], suffix2,
                               c < r, jnp.zeros((m_half, 1), F32), jnp.zeros((m_half, LANES), F32))
    carry = jnp.concatenate([jnp.zeros((m_half, 1), F32), carry_b], axis=0)
    acc = jnp.concatenate([jnp.zeros((m_half, LANES), F32), acc_b], axis=0)
    r = lax.broadcasted_iota(jnp.int32, (2 * m_half, tk), 0)
    c = lax.broadcasted_iota(jnp.int32, (2 * m_half, tk), 1)
    carry, acc = _sb_block(qs, kb[pl.ds(base, tk), :], vb[pl.ds(base, tk), :], suffix2,
                           (r >= m_half) | (c < (r & (tk - 1))), carry, acc)
    carry_ref[...] = carry
    acc_ref[...] = acc

    def left(it, _):
        offs = [pl.multiple_of(base - (2 * it + step) * tk, LANES) for step in (1, 2)]
        z2 = [lax.dot_general(qs, kb[pl.ds(o, tk), :], _NT, preferred_element_type=F32) * LOG2E for o in offs]
        incl = []
        for z in z2:
            neg_abs = lax.bitcast_convert_type(lax.bitcast_convert_type(z, jnp.uint32) | jnp.uint32(0x80000000), F32)
            hi, lo = _split_bf16(jnp.maximum(z, 0.0) + jnp.log2(1.0 + jnp.exp2(neg_abs)))
            incl.append(_dot(jnp.concatenate([hi, lo], axis=1), suffix2))
        carry, acc = carry_ref[...], acc_ref[...]
        for z, inc, o in zip(z2, incl, offs):
            w = jnp.exp2(z - (inc + carry))
            carry = carry + inc[:, 0:1]
            acc = acc + _dot(w.astype(BF16), vb[pl.ds(o, tk), :])
        carry_ref[...] = carry
        acc_ref[...] = acc
        return 0

    lax.fori_loop(0, qi, left, 0)
    c = lax.broadcasted_iota(jnp.int32, (2 * m_half, LANES), 1)
    _, acc = _sb_block(qs, kb[0:LANES, :], vb[0:LANES, :], _suffix_ones(LANES), c >= mpad,
                       carry_ref[...], acc_ref[...])
    for half in range(2):
        for head in range(n_heads):
            rows = (half * n_heads + head) * tk
            o_ref[half * tk:(half + 1) * tk, head * dh:(head + 1) * dh] = (
                acc[rows:rows + tk, head * dh:(head + 1) * dh].astype(o_ref.dtype))


def _sb_sample_body(q_ref, nk_ref, nv_ref, mk_ref, mv_ref, ck_ref, cv_ref, o_ref,
                    qbd, carry_ref, acc_ref, *, ts, n_meta, dh, sub):
    j = pl.program_id(1)
    width = q_ref.shape[1]
    n_heads = width // dh

    def head_mask(shape):
        r = lax.broadcasted_iota(jnp.int32, shape, 0)
        c = lax.broadcasted_iota(jnp.int32, shape, 1)
        return r // ts == c // dh

    def sweep(k, v, mask, carry, acc):
        n = k.shape[0]
        z = lax.dot_general(k.astype(BF16), qbd[...], _NT, preferred_element_type=F32)
        log_beta, log_keep = _sb_scores(z)
        if mask is not None:
            log_keep = jnp.where(mask, log_keep, 0.0)
        r = lax.broadcasted_iota(jnp.int32, (n, n), 0)
        c = lax.broadcasted_iota(jnp.int32, (n, n), 1)
        suffix = jnp.where(c >= r, 1.0, 0.0).astype(BF16)
        hi, lo = _split_bf16(log_keep)
        incl = _dot(suffix, hi) + _dot(suffix, lo)
        w = jnp.exp(log_beta + (carry + incl - log_keep))
        if mask is not None:
            w = jnp.where(mask, w, 0.0)
        acc = acc + lax.dot_general(w.astype(BF16), v.astype(BF16), _TN, preferred_element_type=F32)
        return carry + incl[0:1, :], acc

    @pl.when(j == 0)
    def _():
        qt = jnp.concatenate([q_ref[...]] * n_heads, axis=0)
        qbd[...] = jnp.where(head_mask((n_heads * ts, width)), qt, jnp.zeros_like(qt))
        key = lax.broadcasted_iota(jnp.int32, (ts, LANES), 0)
        qry = lax.broadcasted_iota(jnp.int32, (ts, LANES), 1) % ts
        carry, acc = sweep(nk_ref[...], nv_ref[...], key < qry,
                           jnp.zeros((1, LANES), F32), jnp.zeros((LANES, width), F32))
        carry_ref[...] = carry
        acc_ref[...] = acc

    carry = carry_ref[...]
    acc = acc_ref[...]
    chunk = ck_ref.shape[1]
    for s in range(chunk // sub - 1, -1, -1):
        carry, acc = sweep(ck_ref[0, s * sub:(s + 1) * sub, :], cv_ref[0, s * sub:(s + 1) * sub, :],
                           None, carry, acc)
    carry_ref[...] = carry
    acc_ref[...] = acc

    @pl.when(j == pl.num_programs(1) - 1)
    def _():
        _, acc = sweep(mk_ref[...], mv_ref[...], None, carry_ref[...], acc_ref[...])
        own = jnp.where(head_mask((n_heads * ts, width)), acc, 0.0)
        o_ref[...] = jnp.sum(own.reshape(n_heads, ts, width), axis=0).astype(o_ref.dtype)


def _ret_chunk(q, k, v, state, dec, rowdec, coldec, gc):
    qb = q.astype(BF16)
    inner = lax.dot_general(qb, k.astype(BF16), _NT, preferred_element_type=F32) * dec
    o = _dot(inner.astype(BF16), v) + _dot(qb, state.astype(BF16)) * rowdec
    kw = (k * coldec).astype(BF16)
    return gc * state + lax.dot_general(kw, v, _TN, preferred_element_type=F32), o


def _ret_gate(o, g):
    on = o * lax.rsqrt(jnp.mean(o * o, axis=-1, keepdims=True) + EPS)
    return (g * jax.nn.sigmoid(g)) * on


def _ret_prompt_body(q_ref, k_ref, v_ref, g_ref, mk_ref, mv_ref, dec_ref, rd_ref, cd_ref, gc_ref,
                     mcd_ref, y_ref, s_ref, *, chunk):
    kw = (mk_ref[...] * mcd_ref[0]).astype(BF16)
    state = lax.dot_general(kw, mv_ref[...], _TN, preferred_element_type=F32)
    dec, rd, cd, gc = dec_ref[0], rd_ref[0], cd_ref[0], gc_ref[0]

    def step(c, state):
        r = pl.ds(pl.multiple_of(c * chunk, chunk), chunk)
        state, o = _ret_chunk(q_ref[r, :], k_ref[r, :], v_ref[r, :], state, dec, rd, cd, gc)
        y_ref[r, :] = _ret_gate(o, g_ref[r, :]).astype(y_ref.dtype)
        return state

    s_ref[0, 0] = lax.fori_loop(0, q_ref.shape[0] // chunk, step, state)


def _ret_sample_body(q_ref, k_ref, v_ref, g_ref, s0_ref, dec_ref, rd_ref, cd_ref, gc_ref,
                     y_ref, s_ref):
    state, o = _ret_chunk(q_ref[...], k_ref[...], v_ref[...], s0_ref[0, 0], dec_ref[0], rd_ref[0],
                          cd_ref[0], gc_ref[0])
    y_ref[...] = _ret_gate(o, g_ref[...]).astype(y_ref.dtype)
    s_ref[0, 0] = state


def _pack_bf16_pairs(x):
    bits = lax.bitcast_convert_type(x.astype(BF16).astype(F32), jnp.uint32)
    w = x.shape[1] // 2
    return (bits[:, :w] >> 16) | bits[:, w:]


def _unpack_bf16_pairs(p):
    lo = lax.bitcast_convert_type(p << 16, F32).astype(BF16)
    hi = lax.bitcast_convert_type(p & jnp.uint32(0xFFFF0000), F32).astype(BF16)
    return lo, hi


def _merge_body(*refs, n_prompt, top_k):
    prompt, sample, rest = refs[0:5], refs[5:10], refs[10:]
    i = pl.program_id(0)

    @pl.when(i < n_prompt)
    def _():
        _merge_block(*prompt, *rest, top_k=top_k)

    @pl.when(i >= n_prompt)
    def _():
        _merge_block(*sample, *rest, top_k=top_k)


def _merge_block(ya_ref, yr_ref, ga_ref, gb_ref, x_ref, wa_ref, wb_ref, wo_ref, nf_ref, wr_ref, br_ref,
                 h_ref, tok_ref, ids_ref, gt_ref, cnt_ref, *, top_k):
    mix = (jax.nn.sigmoid(ga_ref[...]) * _dot(ya_ref[...], wa_ref[...])
           + jax.nn.sigmoid(gb_ref[...]) * _dot(yr_ref[...], wb_ref[...]))
    h = x_ref[...] + _dot(mix.astype(BF16), wo_ref[...])
    h_ref[...] = h
    tok = _rms(h, nf_ref[...])
    tok_ref[...] = _pack_bf16_pairs(tok)
    logits = lax.dot_general(wr_ref[...], tok.astype(BF16), _NT, preferred_element_type=F32) + br_ref[...]
    n_exp, tm = logits.shape
    eidx = lax.broadcasted_iota(jnp.int32, (n_exp, tm), 0).astype(F32)
    ids, tops = [], []
    for _ in range(top_k):
        m = jnp.max(logits, axis=0, keepdims=True)
        sel = jnp.min(jnp.where(logits == m, eidx, float(n_exp)), axis=0, keepdims=True)
        ids.append(sel)
        tops.append(m)
        logits = jnp.where(eidx == sel, -jnp.inf, logits)
    ex = [jnp.exp(t - tops[0]) for t in tops]
    inv = 1.0 / functools.reduce(lambda a, b: a + b, ex)
    ids_ref[...] = jnp.concatenate(ids, axis=0).astype(jnp.int32)
    gt_ref[...] = jnp.concatenate([e * inv for e in ex], axis=0)
    cnt = jnp.zeros((n_exp, 1), F32)
    for sel in ids:
        cnt = cnt + jnp.sum(jnp.where(eidx == sel, 1.0, 0.0), axis=1, keepdims=True)
    cnt_ref[0] = cnt


def _local_positions(ids_ref, seg_col, n_exp):
    top_k, tb = ids_ref.shape
    ids = ids_ref[...].astype(F32)
    eidx = lax.broadcasted_iota(jnp.int32, (n_exp, tb), 0).astype(F32)
    r = lax.broadcasted_iota(jnp.int32, (tb, tb), 0)
    c = lax.broadcasted_iota(jnp.int32, (tb, tb), 1)
    before = jnp.where(r < c, 1.0, 0.0).astype(BF16)
    run = seg_col
    rows = []
    for k in range(top_k):
        onehot = jnp.where(ids[k:k + 1, :] == eidx, 1.0, 0.0)
        pos = _dot(onehot.astype(BF16), before) + run
        rows.append(jnp.sum(onehot * pos, axis=0, keepdims=True))
        run = run + jnp.sum(onehot, axis=1, keepdims=True)
    return jnp.concatenate(rows, axis=0)


def _segment_copies(cnt_ref, seg_ref, row_ref, blk, n_exp, max_rows, vmem, hbm, sem, *, to_hbm, wait):
    del max_rows

    def window(src0, dst0, off, size):
        v = vmem.at[pl.ds(pl.multiple_of(src0 + off, SEG_ALIGN), size), :]
        h = hbm.at[pl.ds(pl.multiple_of(dst0 + off, SEG_ALIGN), size), :]
        cp = pltpu.make_async_copy(v, h, sem) if to_hbm else pltpu.make_async_copy(h, v, sem)
        if wait:
            cp.wait()
        else:
            cp.start()

    def per_expert(e, _):
        n = cnt_ref[blk * n_exp + e]
        src0 = seg_ref[blk * n_exp + e]
        dst0 = row_ref[blk * n_exp + e]
        n_big = n // SEG_BIG

        def big(j, _):
            window(src0, dst0, j * SEG_BIG, SEG_BIG)
            return 0

        lax.fori_loop(0, n_big, big, 0)
        off = n_big * SEG_BIG
        for bit in range(SEG_BIG.bit_length() - 2, SEG_ALIGN.bit_length() - 2, -1):
            size = 1 << bit
            take = (n >> bit) & 1

            @pl.when(take == 1)
            def _(off=off, size=size):
                window(src0, dst0, off, size)

            off = off + take * size
        return 0

    lax.fori_loop(0, n_exp, per_expert, 0)


def _permutation(pos, n_rows):
    p = lax.broadcasted_iota(jnp.int32, (n_rows, pos.shape[1]), 0)
    out = jnp.where(pos[0:1, :] == p, 1.0, 0.0)
    for k in range(1, pos.shape[0]):
        out = out + jnp.where(pos[k:k + 1, :] == p, 1.0, 0.0)
    return out


def _dispatch_body(cnt_ref, seg_ref, row_ref, ids_ref, gt_ref, segc_ref, tok_ref, xs_out, pos_ref,
                   loc, zeros, sems, *, n_exp):
    blk = pl.program_id(0)
    last = pl.num_programs(0) - 1
    slot = blk % 2
    top_k, tb = ids_ref.shape
    zb = zeros.shape[0]
    words = tok_ref.shape[1]
    n_rows = loc.shape[1]

    def copies(b, s, wait):
        _segment_copies(cnt_ref, seg_ref, row_ref, b, n_exp, n_rows, loc.at[s], xs_out, sems.at[s],
                        to_hbm=True, wait=wait)

    @pl.when(blk == 0)
    def _():
        zeros[...] = jnp.zeros_like(zeros)

        def fill(b, _):
            pltpu.make_async_copy(zeros, xs_out.at[pl.ds(pl.multiple_of(b * zb, zb), zb), :], sems.at[0]).start()
            return 0

        def done(b, _):
            pltpu.make_async_copy(zeros, xs_out.at[pl.ds(pl.multiple_of(b * zb, zb), zb), :], sems.at[0]).wait()
            return 0

        lax.fori_loop(0, xs_out.shape[0] // zb, fill, 0)
        lax.fori_loop(0, xs_out.shape[0] // zb, done, 0)

    @pl.when(blk >= 2)
    def _():
        copies(blk - 2, slot, True)

    pos = _local_positions(ids_ref, segc_ref[0], n_exp).astype(jnp.int32)
    pos_ref[...] = pos
    perm = _permutation(pos, n_rows).astype(BF16)
    lo, hi = _unpack_bf16_pairs(tok_ref[...])
    lo_bits = lax.bitcast_convert_type(_dot(perm, lo), jnp.uint32)
    hi_bits = lax.bitcast_convert_type(_dot(perm, hi), jnp.uint32)
    buf = loc.at[slot]
    buf[:, 0:words] = (lo_bits >> 16) | hi_bits
    p = lax.broadcasted_iota(jnp.int32, (n_rows, tb), 0)
    gates = gt_ref[...]
    spread = jnp.where(pos[0:1, :] == p, gates[0:1, :], 0.0)
    for k in range(1, top_k):
        spread = spread + jnp.where(pos[k:k + 1, :] == p, gates[k:k + 1, :], 0.0)
    gate_row = jnp.sum(spread, axis=1, keepdims=True)
    buf[:, words:] = lax.bitcast_convert_type(jnp.broadcast_to(gate_row, (n_rows, LANES)), jnp.uint32)
    copies(blk, slot, False)

    @pl.when(blk == last)
    def _():
        @pl.when(blk >= 1)
        def _():
            copies(blk - 1, 1 - slot, True)

        copies(blk, slot, True)


def _expert_body(be_ref, nu_ref, x_ref, wgu_ref, bgu_ref, wd_ref, bd_ref, o_ref, wgu_b, wd_b, *, d_ff):
    i = pl.program_id(0)
    used = i < nu_ref[0]

    @pl.when(jnp.logical_and(used, jnp.logical_or(i == 0, be_ref[i] != be_ref[jnp.maximum(i - 1, 0)])))
    def _():
        wgu_b[...] = wgu_ref[0].astype(BF16)
        wd_b[...] = wd_ref[0].astype(BF16)

    @pl.when(used)
    def _():
        x = x_ref[...]
        half = x.shape[1] - LANES
        lo, hi = _unpack_bf16_pairs(x[:, :half])
        gate = lax.bitcast_convert_type(x[:, half:half + 1], F32)
        h = _dot(lo, wgu_b[:half, :]) + _dot(hi, wgu_b[half:, :]) + bgu_ref[0]
        g = jnp.minimum(h[:, :d_ff], SWIGLU_LIMIT)
        u = jnp.clip(h[:, d_ff:], -SWIGLU_LIMIT, SWIGLU_LIMIT)
        act = (u + 1.0) * (g * jax.nn.sigmoid(SWIGLU_ALPHA * g))
        out = (_dot(act.astype(BF16), wd_b[...]) + bd_ref[0]) * gate
        hi_part, lo_part = _split_bf16(out)
        o_ref[...] = (lax.bitcast_convert_type(hi_part.astype(F32), jnp.uint32)
                      | (lax.bitcast_convert_type(lo_part.astype(F32), jnp.uint32) >> 16))

    @pl.when(jnp.logical_not(used))
    def _():
        o_ref[...] = jnp.zeros_like(o_ref)


def _combine_body(cnt_ref, seg_ref, row_ref, pos_ref, h_ref, nf_ref, os_hbm, y_ref, loc, sems, *, n_exp, blk_off):
    i = pl.program_id(0)
    slot = i % 2
    n_rows = loc.shape[1]
    n_pairs = pos_ref.shape[0] * pos_ref.shape[1]

    def fetch(step, s, wait):
        _segment_copies(cnt_ref, seg_ref, row_ref, step + blk_off, n_exp, n_rows, loc.at[s], os_hbm, sems.at[s],
                        to_hbm=False, wait=wait)

    def prepare(step, s):
        loc[s, n_pairs:, :] = jnp.zeros((n_rows - n_pairs, loc.shape[2]), loc.dtype)
        fetch(step, s, False)

    @pl.when(i == 0)
    def _():
        prepare(0, 0)

    @pl.when(i + 1 < pl.num_programs(0))
    def _():
        prepare(i + 1, 1 - slot)

    perm = _permutation(pos_ref[...], n_rows).astype(BF16)
    fetch(i, slot, True)
    w = loc[slot]
    hi_part = lax.bitcast_convert_type(w & jnp.uint32(0xFFFF0000), F32).astype(BF16)
    lo_part = lax.bitcast_convert_type(w << 16, F32).astype(BF16)
    f = (lax.dot_general(perm, hi_part, _TN, preferred_element_type=F32)
         + lax.dot_general(perm, lo_part, _TN, preferred_element_type=F32))
    y_ref[...] = _rms(h_ref[...] + f, nf_ref[...])


def _largest_pow2_divisor(n, cap):
    t = cap
    while n % t:
        t //= 2
    return t


def _rotation_tables(pos, half):
    inv_freq = 1.0 / (ROPE_BASE ** jnp.linspace(0.0, 1.0, half, dtype=F32))
    ang = pos.astype(F32)[:, None] * inv_freq[None, :]
    cos, sin = jnp.cos(ang), jnp.sin(ang)
    return jnp.concatenate([cos, cos], axis=1), jnp.concatenate([-sin, sin], axis=1)


def _decay_tables(log_g, c):
    idx = jnp.arange(c, dtype=F32)
    lg = log_g[:, None]
    rel = idx[:, None] - idx[None, :]
    dec = jnp.where(rel >= 0, jnp.exp(lg[:, :, None] * jnp.maximum(rel, 0.0)), 0.0)
    rowdec = jnp.exp(lg * (idx + 1.0))[:, :, None]
    coldec = jnp.exp(lg * (c - 1.0 - idx))[:, :, None]
    gc = jnp.exp(lg * c)[:, :, None]
    return dec, rowdec, coldec, gc


def kernel(x_prompt, x_sample, cache_sb_k, cache_sb_v, state_ret, meta_tokens, norm_mix, w_in, w_branch_sb, w_branch_ret, w_out, norm_ffn, w_router, b_router, w_gate_up, b_gate_up, w_down, b_down, norm_final):
    B, SEQ, D = x_prompt.shape
    Bs, Ts, _ = x_sample.shape
    depth, _, past, h_sb, dh = cache_sb_k.shape
    _, _, h_ret, dk, dv = state_ret.shape
    n_meta = meta_tokens.shape[0]
    n_exp = w_router.shape[-1]
    d_ff = w_down.shape[-2]
    sbw, rqk, rv = h_sb * dh, h_ret * dk, h_ret * dv
    assert depth == 1, "one layer: the meta-token rows feed nothing after the mixers"
    assert dk == LANES and 2 * dh == LANES and h_sb * Ts == LANES and n_meta % 16 == 0

    TQ = _largest_pow2_divisor(SEQ, 512)
    assert TQ >= 2 * LANES and n_meta <= LANES
    t_p, t_s = B * SEQ, Bs * Ts
    t_moe = t_p + t_s
    TM = _largest_pow2_divisor(t_moe, 512)
    assert TM >= 16 and SEQ % TM == 0 and t_p % TM == 0 and t_moe % n_meta == 0 and t_p % Ts == 0
    n_p, n_moe = t_p // TM, t_moe // TM
    t_x = t_s + n_meta
    meta_blk = t_s // n_meta

    x_p = x_prompt.reshape(t_p, D)
    x_x = jnp.concatenate([x_sample.reshape(t_s, D), meta_tokens.astype(F32)], axis=0)

    sizes = (sbw,) * 3 + (rqk,) * 2 + (rv,) * 2 + (D,) * 2
    offs = tuple(sum(sizes[:i]) for i in range(len(sizes) + 1))
    n_sq = SEQ // TM
    row = lambda w: pl.BlockSpec((TM, w), lambda i: (i, 0))
    out_dt = (BF16, F32, F32, F32, F32, BF16, F32, F32, F32)
    w_in_b, g_mix = w_in[0].astype(BF16), norm_mix[0][None, :]

    def inproj(x, pos, tm, tab_period):
        rows = x.shape[0]
        blk = lambda w: pl.BlockSpec((tm, w), lambda i: (i, 0))
        tab = pl.BlockSpec((tm, dk), lambda i: (i % tab_period, 0))
        return pl.pallas_call(
            functools.partial(_inproj_body, offs=offs, q_scale=dh ** -0.5, k_scale=dk ** -0.5, n_ret=h_ret, dk=dk),
            grid=(rows // tm,),
            in_specs=[blk(D), pl.BlockSpec((1, D), lambda i: (0, 0)),
                      pl.BlockSpec((D, offs[-1]), lambda i: (0, 0), pipeline_mode=pl.Buffered(1)), tab, tab],
            out_specs=[blk(w) for w in sizes],
            out_shape=[jax.ShapeDtypeStruct((rows, w), dt) for w, dt in zip(sizes, out_dt)],
            compiler_params=_params("parallel"), name="inproj",
        )(x, g_mix, w_in_b, *_rotation_tables(pos, dk // 2))

    qa, ka, va, qr, kr, vr, gr, ga, gb = inproj(x_p, jnp.arange(SEQ), TM, n_sq)
    x_pos = jnp.concatenate([jnp.tile(past + jnp.arange(Ts), Bs), jnp.arange(n_meta) - n_meta])
    qa_x, ka_x, va_x, qr_x, kr_x, vr_x, gr_x, ga_x, gb_x = inproj(x_x, x_pos, t_x, 1)

    n_q = SEQ // TQ
    n_pair = sbw // LANES
    ya_p = pl.pallas_call(
        functools.partial(_sb_prompt_body, tq=TQ, n_meta=n_meta, dh=dh),
        grid=(B, n_pair, n_q),
        in_specs=[pl.BlockSpec((TQ, LANES), lambda b, p, q: (b * n_q + q, p)),
                  pl.BlockSpec((SEQ, LANES), lambda b, p, q: (b, p)),
                  pl.BlockSpec((SEQ, LANES), lambda b, p, q: (b, p)),
                  pl.BlockSpec((n_meta, LANES), lambda b, p, q: (meta_blk, p)),
                  pl.BlockSpec((n_meta, LANES), lambda b, p, q: (meta_blk, p))],
        out_specs=pl.BlockSpec((TQ, LANES), lambda b, p, q: (b * n_q + q, p)),
        out_shape=jax.ShapeDtypeStruct((t_p, sbw), BF16),
        scratch_shapes=[pltpu.VMEM((LANES + SEQ, LANES), BF16), pltpu.VMEM((LANES + SEQ, LANES), BF16),
                        pltpu.VMEM((2 * TQ, 1), F32), pltpu.VMEM((2 * TQ, LANES), F32)],
        compiler_params=_params("parallel", "parallel", "arbitrary"), name="sb_prompt",
    )(qa, ka, va, ka_x, va_x)

    KC = _largest_pow2_divisor(past, 512)
    SUB = min(KC, 256)
    n_kc = past // KC
    smp = lambda w: pl.BlockSpec((Ts, w), lambda b, j: (b, 0))
    met = lambda w: pl.BlockSpec((n_meta, w), lambda b, j: (meta_blk, 0))
    cache = pl.BlockSpec((1, KC, sbw), lambda b, j: (b, n_kc - 1 - j, 0))
    ya_s = pl.pallas_call(
        functools.partial(_sb_sample_body, ts=Ts, n_meta=n_meta, dh=dh, sub=SUB),
        grid=(Bs, n_kc),
        in_specs=[smp(sbw), smp(sbw), smp(sbw), met(sbw), met(sbw), cache, cache],
        out_specs=pl.BlockSpec((Ts, sbw), lambda b, j: (b, 0)),
        out_shape=jax.ShapeDtypeStruct((t_s, sbw), BF16),
        scratch_shapes=[pltpu.VMEM((LANES, sbw), BF16), pltpu.VMEM((1, LANES), F32), pltpu.VMEM((LANES, sbw), F32)],
        compiler_params=_params("parallel", "arbitrary"), name="sb_sample",
    )(qa_x, ka_x, va_x, ka_x, va_x, cache_sb_k[0].reshape(Bs, past, sbw), cache_sb_v[0].reshape(Bs, past, sbw))

    log_g = jnp.log(1.0 - 2.0 ** (-5.0 - jnp.arange(h_ret, dtype=F32)))
    RC = _largest_pow2_divisor(SEQ, 256)
    dec, rd, cd, gc = _decay_tables(log_g, RC)
    _, _, mcd, _ = _decay_tables(log_g, n_meta)
    per_b = lambda w: pl.BlockSpec((SEQ, w), lambda b, h: (b, h))
    met_h = lambda w: pl.BlockSpec((n_meta, w), lambda b, h: (meta_blk, h))
    tab = lambda a: pl.BlockSpec((1,) + a.shape[1:], lambda b, h: (h, 0, 0))
    state_spec = pl.BlockSpec((1, 1, dk, dv), lambda b, h: (b, h, 0, 0))
    yr_p, s_p = pl.pallas_call(
        functools.partial(_ret_prompt_body, chunk=RC),
        grid=(B, h_ret),
        in_specs=[per_b(dk), per_b(dk), per_b(dv), per_b(dv), met_h(dk), met_h(dv),
                  tab(dec), tab(rd), tab(cd), tab(gc), tab(mcd)],
        out_specs=[per_b(dv), state_spec],
        out_shape=[jax.ShapeDtypeStruct((t_p, rv), BF16), jax.ShapeDtypeStruct((B, h_ret, dk, dv), F32)],
        compiler_params=_params("parallel", "parallel"), name="ret_prompt",
    )(qr, kr, vr, gr, kr_x, vr_x, dec, rd, cd, gc, mcd)

    dec_s, rd_s, cd_s, gc_s = _decay_tables(log_g, Ts)
    smp_h = lambda w: pl.BlockSpec((Ts, w), lambda b, h: (b, h))
    yr_s, s_s = pl.pallas_call(
        _ret_sample_body,
        grid=(Bs, h_ret),
        in_specs=[smp_h(dk), smp_h(dk), smp_h(dv), smp_h(dv), state_spec,
                  tab(dec_s), tab(rd_s), tab(cd_s), tab(gc_s)],
        out_specs=[smp_h(dv), state_spec],
        out_shape=[jax.ShapeDtypeStruct((t_s, rv), BF16), jax.ShapeDtypeStruct((Bs, h_ret, dk, dv), F32)],
        compiler_params=_params("parallel", "parallel"), name="ret_sample",
    )(qr_x, kr_x, vr_x, gr_x, state_ret[0], dec_s, rd_s, cd_s, gc_s)

    const = lambda a: pl.BlockSpec(a.shape, lambda i: (0,) * a.ndim)
    prm = lambda w: pl.BlockSpec((TM, w), lambda i: (jnp.minimum(i, n_p - 1), 0))
    smp_m = lambda w: pl.BlockSpec((TM, w), lambda i: (jnp.maximum(i - n_p, 0), 0))
    lane_major = pl.BlockSpec((TOP_K, TM), lambda i: (0, i))
    wa, wb, wo = w_branch_sb[0].astype(BF16), w_branch_ret[0].astype(BF16), w_out[0].astype(BF16)
    nf, wr_t, br = norm_ffn[0][None, :], w_router[0].T.astype(BF16), b_router[0][:, None]
    h1, tok, ids, gates, cnt = pl.pallas_call(
        functools.partial(_merge_body, n_prompt=n_p, top_k=TOP_K),
        grid=(n_moe,),
        in_specs=[prm(sbw), prm(rv), prm(D), prm(D), prm(D), smp_m(sbw), smp_m(rv), smp_m(D), smp_m(D), smp_m(D),
                  const(wa), const(wb), const(wo), const(nf), const(wr_t), const(br)],
        out_specs=[row(D), row(D // 2), lane_major, lane_major, pl.BlockSpec((1, n_exp, 1), lambda i: (i, 0, 0))],
        out_shape=[jax.ShapeDtypeStruct((t_moe, D), F32), jax.ShapeDtypeStruct((t_moe, D // 2), jnp.uint32),
                   jax.ShapeDtypeStruct((TOP_K, t_moe), jnp.int32), jax.ShapeDtypeStruct((TOP_K, t_moe), F32),
                   jax.ShapeDtypeStruct((n_moe, n_exp, 1), F32)],
        compiler_params=_params("parallel"), name="merge_router",
    )(ya_p, yr_p, ga, gb, x_p, ya_s, yr_s, ga_x, gb_x, x_x, wa, wb, wo, nf, wr_t, br)

    BM = TM
    cnt = cnt[:, :, 0].astype(jnp.int32)
    cnt = (cnt + SEG_ALIGN - 1) // SEG_ALIGN * SEG_ALIGN
    padded = (jnp.sum(cnt, axis=0) + BM - 1) // BM * BM
    pad_end = jnp.cumsum(padded)
    seg_row = (pad_end - padded)[None, :] + jnp.cumsum(cnt, axis=0) - cnt
    seg_loc = jnp.cumsum(cnt, axis=1) - cnt
    n_blk = -(-(t_moe * TOP_K + (SEG_ALIGN - 1) * n_moe * n_exp) // BM) + n_exp
    block_e = jnp.minimum(jnp.sum(pad_end[None, :] <= (jnp.arange(n_blk) * BM)[:, None], axis=1), n_exp - 1).astype(jnp.int32)
    n_used = (pad_end[-1:] // BM).astype(jnp.int32)
    seg_tabs = (cnt.reshape(-1), seg_loc.reshape(-1).astype(jnp.int32), seg_row.reshape(-1).astype(jnp.int32))

    XW = D // 2 + LANES
    n_loc = -(-(TOP_K * TM + SEG_ALIGN * n_exp) // LANES) * LANES
    xs, pos = pl.pallas_call(
        functools.partial(_dispatch_body, n_exp=n_exp),
        grid_spec=pltpu.PrefetchScalarGridSpec(
            num_scalar_prefetch=3, grid=(n_moe,),
            in_specs=[pl.BlockSpec((TOP_K, TM), lambda i, *_: (0, i)), pl.BlockSpec((TOP_K, TM), lambda i, *_: (0, i)),
                      pl.BlockSpec((1, n_exp, 1), lambda i, *_: (i, 0, 0)),
                      pl.BlockSpec((TM, D // 2), lambda i, *_: (i, 0))],
            out_specs=[pl.BlockSpec(memory_space=pl.ANY), pl.BlockSpec((TOP_K, TM), lambda i, *_: (0, i))],
            scratch_shapes=[pltpu.VMEM((2, n_loc, XW), jnp.uint32), pltpu.VMEM((BM, XW), jnp.uint32),
                            pltpu.SemaphoreType.DMA((2,))]),
        out_shape=[jax.ShapeDtypeStruct((n_blk * BM, XW), jnp.uint32), jax.ShapeDtypeStruct((TOP_K, t_moe), jnp.int32)],
        compiler_params=_params("arbitrary"), name="moe_dispatch",
    )(*seg_tabs, ids, gates, seg_loc.astype(F32)[:, :, None], tok)

    blk = lambda i, be, nu: jnp.minimum(i, nu[0] - 1)
    exp_w = lambda a: pl.BlockSpec((1,) + a.shape[1:], lambda i, be, nu: (be[blk(i, be, nu)],) + (0,) * (a.ndim - 1))
    wgu, bgu = w_gate_up[0], b_gate_up[0][:, None, :]
    wd, bd = w_down[0], b_down[0][:, None, :]
    os_ = pl.pallas_call(
        functools.partial(_expert_body, d_ff=d_ff),
        grid_spec=pltpu.PrefetchScalarGridSpec(
            num_scalar_prefetch=2, grid=(n_blk,),
            in_specs=[pl.BlockSpec((BM, XW), lambda i, be, nu: (blk(i, be, nu), 0)),
                      exp_w(wgu), exp_w(bgu), exp_w(wd), exp_w(bd)],
            out_specs=pl.BlockSpec((BM, D), lambda i, be, nu: (i, 0)),
            scratch_shapes=[pltpu.VMEM(wgu.shape[1:], BF16), pltpu.VMEM(wd.shape[1:], BF16)]),
        out_shape=jax.ShapeDtypeStruct((n_blk * BM, D), jnp.uint32),
        compiler_params=_params("arbitrary"), name="moe_experts",
    )(block_e, n_used, xs, wgu, bgu, wd, bd)

    def combine(off, n):
        return pl.pallas_call(
            functools.partial(_combine_body, n_exp=n_exp, blk_off=off),
            grid_spec=pltpu.PrefetchScalarGridSpec(
                num_scalar_prefetch=3, grid=(n,),
                in_specs=[pl.BlockSpec((TOP_K, TM), lambda i, *_: (0, i + off)),
                          pl.BlockSpec((TM, D), lambda i, *_: (i + off, 0)),
                          pl.BlockSpec((1, D), lambda i, *_: (0, 0)), pl.BlockSpec(memory_space=pl.ANY)],
                out_specs=pl.BlockSpec((TM, D), lambda i, *_: (i, 0)),
                scratch_shapes=[pltpu.VMEM((2, n_loc, D), jnp.uint32), pltpu.SemaphoreType.DMA((2,))]),
            out_shape=jax.ShapeDtypeStruct((n * TM, D), F32),
            compiler_params=_params("arbitrary"), name="moe_combine",
        )(*seg_tabs, pos, h1, norm_final[None, :], os_)

    y_prompt = combine(0, n_p).reshape(B, SEQ, D)
    y_sample = combine(n_p, n_moe - n_p).reshape(Bs, Ts, D)

    def with_meta(a, a_x):
        meta_rows = jnp.broadcast_to(a_x[t_s:].reshape(1, n_meta, h_sb, dh), (B, n_meta, h_sb, dh))
        return jnp.concatenate([meta_rows, a.reshape(B, SEQ, h_sb, dh)], axis=1)[None]

    return (y_prompt, y_sample, with_meta(ka, ka_x), with_meta(va, va_x), s_p[None],
            ka_x[:t_s].reshape(1, Bs, Ts, h_sb, dh), va_x[:t_s].reshape(1, Bs, Ts, h_sb, dh), s_s[None])
```

```python
import functools

import jax
import jax.numpy as jnp
from jax import lax
from jax.experimental import pallas as pl
from jax.experimental.pallas import tpu as pltpu

F32 = jnp.float32
BF16 = jnp.bfloat16

EPS = 1e-6
ROPE_BASE = 10000.0
TOP_K = 4
SWIGLU_LIMIT = 7.0
SWIGLU_ALPHA = 1.702
LOG2E = 1.4426950408889634

LANES = 128
SEG_ALIGN = 8
SEG_BIG = 256
VMEM_LIMIT = 56 << 20

_NT = (((1,), (1,)), ((), ()))
_TN = (((0,), (0,)), ((), ()))


def _params(*sem, **kw):
    return pltpu.CompilerParams(dimension_semantics=sem, vmem_limit_bytes=VMEM_LIMIT, **kw)


def _dot(a, b):
    return jnp.dot(a, b, preferred_element_type=F32)


def _rms(x, g):
    return x * lax.rsqrt(jnp.mean(x * x, axis=-1, keepdims=True) + EPS) * g


def _inproj_body(x_ref, g_ref, w_ref, cs_ref, sn_ref, qa_ref, ka_ref, va_ref, qr_ref, kr_ref,
                 vr_ref, gr_ref, ga_ref, gb_ref, *, offs, q_scale, k_scale, n_ret, dk):
    xn = _rms(x_ref[...], g_ref[...]).astype(BF16)

    def proj(i):
        return _dot(xn, w_ref[:, offs[i]:offs[i + 1]])

    qa_ref[...] = (proj(0) * q_scale).astype(BF16)
    ka_ref[...] = proj(1)
    va_ref[...] = proj(2)
    cs = cs_ref[...]
    sn = sn_ref[...]
    for ref, i, scale in ((qr_ref, 3, None), (kr_ref, 4, k_scale)):
        t = proj(i)
        for h in range(n_ret):
            th = t[:, h * dk:(h + 1) * dk]
            r = th * cs + pltpu.roll(th, dk // 2, axis=1) * sn
            ref[:, h * dk:(h + 1) * dk] = r if scale is None else r * scale
    vr_ref[...] = proj(5).astype(BF16)
    gr_ref[...] = proj(6)
    ga_ref[...] = proj(7)
    gb_ref[...] = proj(8)


def _sb_scores(z):
    log_beta = jnp.minimum(z, 0.0) - jnp.log(1.0 + jnp.exp(-jnp.abs(z)))
    return log_beta, log_beta - z


def _split_bf16(x):
    hi = x.astype(BF16)
    return hi, (x - hi.astype(F32)).astype(BF16)


def _suffix_ones(n):
    r = lax.broadcasted_iota(jnp.int32, (2 * n, n), 0) & (n - 1)
    c = lax.broadcasted_iota(jnp.int32, (2 * n, n), 1)
    return jnp.where(r >= c, 1.0, 0.0).astype(BF16)


def _sb_logits(qs, kblk):
    return lax.dot_general(qs, kblk, _NT, preferred_element_type=F32) * LOG2E


def _sb_suffix(z2, suffix2, mask):
    neg_abs = lax.bitcast_convert_type(lax.bitcast_convert_type(z2, jnp.uint32) | jnp.uint32(0x80000000), F32)
    drop = jnp.maximum(z2, 0.0) + jnp.log2(1.0 + jnp.exp2(neg_abs))
    if mask is not None:
        drop = jnp.where(mask, drop, 0.0)
    hi, lo = _split_bf16(drop)
    return _dot(jnp.concatenate([hi, lo], axis=1), suffix2)


def _sb_weights(z2, incl, carry, mask):
    w = jnp.exp2(z2 - (incl if carry is None else incl + carry))
    if mask is not None:
        w = jnp.where(mask, w, 0.0)
    return w.astype(BF16)


def _sb_block(qs, kblk, vblk, suffix2, mask, carry, acc):
    z2 = _sb_logits(qs, kblk)
    incl = _sb_suffix(z2, suffix2, mask)
    return carry + incl[:, 0:1], acc + _dot(_sb_weights(z2, incl, carry, mask), vblk)


def _sb_prompt_body(q_ref, k_ref, v_ref, mk_ref, mv_ref, o_ref, kb, vb, carry_ref, acc_ref, *, tq, n_meta, dh):
    qi = pl.program_id(2)
    tk = tq // 2
    mpad = LANES - n_meta
    n_heads = LANES // dh

    @pl.when(qi == 0)
    def _():
        kb[0:mpad, :] = jnp.zeros((mpad, LANES), BF16)
        vb[0:mpad, :] = jnp.zeros((mpad, LANES), BF16)
        kb[mpad:LANES, :] = mk_ref[...].astype(BF16)
        vb[mpad:LANES, :] = mv_ref[...].astype(BF16)
        kb[LANES:, :] = k_ref[...].astype(BF16)
        vb[LANES:, :] = v_ref[...].astype(BF16)

    q = q_ref[...]
    lane = lax.broadcasted_iota(jnp.int32, (tk, LANES), 1)
    parts = []
    for half in range(2):
        qh = q[half * tk:(half + 1) * tk]
        for head in range(n_heads):
            parts.append(jnp.where(lane // dh == head, qh, jnp.zeros_like(qh)))
    qs = jnp.concatenate(parts, axis=0)
    m_half = n_heads * tk
    suffix2 = _suffix_ones(tk)
    base = pl.multiple_of(LANES + qi * tq, LANES)

    r = lax.broadcasted_iota(jnp.int32, (m_half, tk), 0) & (tk - 1)
    c = lax.broadcasted_iota(jnp.int32, (m_half, tk), 1)
    mask_b = c < r
    r = lax.broadcasted_iota(jnp.int32, (2 * m_half, tk), 0)
    c = lax.broadcasted_iota(jnp.int32, (2 * m_half, tk), 1)
    mask_a = (r >= m_half) | (c < (r & (tk - 1)))
    z_b = _sb_logits(qs[m_half:], kb[pl.ds(base + tk, tk), :])
    z_a = _sb_logits(qs, kb[pl.ds(base, tk), :])
    incl_b = _sb_suffix(z_b, suffix2, mask_b)
    incl_a = _sb_suffix(z_a, suffix2, mask_a)
    acc_b = _dot(_sb_weights(z_b, incl_b, None, mask_b), vb[pl.ds(base + tk, tk), :])
    carry = jnp.concatenate([jnp.zeros((m_half, 1), F32), incl_b[:, 0:1]], axis=0)
    acc = jnp.concatenate([jnp.zeros((m_half, LANES), F32), acc_b], axis=0)
    acc_ref[...] = acc + _dot(_sb_weights(z_a, incl_a, carry, mask_a), vb[pl.ds(base, tk), :])
    carry_ref[...] = carry + incl_a[:, 0:1]

    def left(it, _):
        offs = [pl.multiple_of(base - (2 * it + step) * tk, LANES) for step in (1, 2)]
        z2 = [_sb_logits(qs, kb[pl.ds(o, tk), :]) for o in offs]
        incl = [_sb_suffix(z, suffix2, None) for z in z2]
        carry, acc = carry_ref[...], acc_ref[...]
        for z, inc, o in zip(z2, incl, offs):
            acc = acc + _dot(_sb_weights(z, inc, carry, None), vb[pl.ds(o, tk), :])
            carry = carry + inc[:, 0:1]
        carry_ref[...] = carry
        acc_ref[...] = acc
        return 0

    lax.fori_loop(0, qi, left, 0)
    c = lax.broadcasted_iota(jnp.int32, (2 * m_half, LANES), 1)
    _, acc = _sb_block(qs, kb[0:LANES, :], vb[0:LANES, :], _suffix_ones(LANES), c >= mpad,
                       carry_ref[...], acc_ref[...])
    for half in range(2):
        for head in range(n_heads):
            rows = (half * n_heads + head) * tk
            o_ref[half * tk:(half + 1) * tk, head * dh:(head + 1) * dh] = (
                acc[rows:rows + tk, head * dh:(head + 1) * dh].astype(o_ref.dtype))


def _sb_sample_body(q_ref, nk_ref, nv_ref, mk_ref, mv_ref, ck_ref, cv_ref, o_ref,
                    qbd, carry_ref, acc_ref, *, ts, n_meta, dh, sub):
    j = pl.program_id(1)
    width = q_ref.shape[1]
    n_heads = width // dh

    def head_mask(shape):
        r = lax.broadcasted_iota(jnp.int32, shape, 0)
        c = lax.broadcasted_iota(jnp.int32, shape, 1)
        return r // ts == c // dh

    def sweep(k, v, mask, carry, acc):
        n = k.shape[0]
        z = lax.dot_general(k.astype(BF16), qbd[...], _NT, preferred_element_type=F32)
        log_beta, log_keep = _sb_scores(z)
        if mask is not None:
            log_keep = jnp.where(mask, log_keep, 0.0)
        r = lax.broadcasted_iota(jnp.int32, (n, n), 0)
        c = lax.broadcasted_iota(jnp.int32, (n, n), 1)
        suffix = jnp.where(c >= r, 1.0, 0.0).astype(BF16)
        hi, lo = _split_bf16(log_keep)
        incl = _dot(suffix, hi) + _dot(suffix, lo)
        w = jnp.exp(log_beta + (carry + incl - log_keep))
        if mask is not None:
            w = jnp.where(mask, w, 0.0)
        acc = acc + lax.dot_general(w.astype(BF16), v.astype(BF16), _TN, preferred_element_type=F32)
        return carry + incl[0:1, :], acc

    @pl.when(j == 0)
    def _():
        qt = jnp.concatenate([q_ref[...]] * n_heads, axis=0)
        qbd[...] = jnp.where(head_mask((n_heads * ts, width)), qt, jnp.zeros_like(qt))
        key = lax.broadcasted_iota(jnp.int32, (ts, LANES), 0)
        qry = lax.broadcasted_iota(jnp.int32, (ts, LANES), 1) % ts
        carry, acc = sweep(nk_ref[...], nv_ref[...], key < qry,
                           jnp.zeros((1, LANES), F32), jnp.zeros((LANES, width), F32))
        carry_ref[...] = carry
        acc_ref[...] = acc

    carry = carry_ref[...]
    acc = acc_ref[...]
    chunk = ck_ref.shape[1]
    for s in range(chunk // sub - 1, -1, -1):
        carry, acc = sweep(ck_ref[0, s * sub:(s + 1) * sub, :], cv_ref[0, s * sub:(s + 1) * sub, :],
                           None, carry, acc)
    carry_ref[...] = carry
    acc_ref[...] = acc

    @pl.when(j == pl.num_programs(1) - 1)
    def _():
        _, acc = sweep(mk_ref[...], mv_ref[...], None, carry_ref[...], acc_ref[...])
        own = jnp.where(head_mask((n_heads * ts, width)), acc, 0.0)
        o_ref[...] = jnp.sum(own.reshape(n_heads, ts, width), axis=0).astype(o_ref.dtype)


def _ret_chunk(q, k, v, state, dec, rowdec, coldec, gc):
    qb = q.astype(BF16)
    inner = lax.dot_general(qb, k.astype(BF16), _NT, preferred_element_type=F32) * dec
    o = _dot(inner.astype(BF16), v) + _dot(qb, state.astype(BF16)) * rowdec
    kw = (k * coldec).astype(BF16)
    return gc * state + lax.dot_general(kw, v, _TN, preferred_element_type=F32), o


def _ret_gate(o, g):
    on = o * lax.rsqrt(jnp.mean(o * o, axis=-1, keepdims=True) + EPS)
    return (g * jax.nn.sigmoid(g)) * on


def _ret_prompt_body(q_ref, k_ref, v_ref, g_ref, mk_ref, mv_ref, dec_ref, rd_ref, cd_ref, gc_ref,
                     mcd_ref, y_ref, s_ref, *, chunk):
    kw = (mk_ref[...] * mcd_ref[0]).astype(BF16)
    state = lax.dot_general(kw, mv_ref[...], _TN, preferred_element_type=F32)
    dec, rd, cd, gc = dec_ref[0], rd_ref[0], cd_ref[0], gc_ref[0]

    def step(c, state):
        r = pl.ds(pl.multiple_of(c * chunk, chunk), chunk)
        state, o = _ret_chunk(q_ref[r, :], k_ref[r, :], v_ref[r, :], state, dec, rd, cd, gc)
        y_ref[r, :] = _ret_gate(o, g_ref[r, :]).astype(y_ref.dtype)
        return state

    s_ref[0, 0] = lax.fori_loop(0, q_ref.shape[0] // chunk, step, state)


def _ret_sample_body(q_ref, k_ref, v_ref, g_ref, s0_ref, dec_ref, rd_ref, cd_ref, gc_ref,
                     y_ref, s_ref):
    state, o = _ret_chunk(q_ref[...], k_ref[...], v_ref[...], s0_ref[0, 0], dec_ref[0], rd_ref[0],
                          cd_ref[0], gc_ref[0])
    y_ref[...] = _ret_gate(o, g_ref[...]).astype(y_ref.dtype)
    s_ref[0, 0] = state


def _pack_bf16_pairs(x):
    bits = lax.bitcast_convert_type(x.astype(BF16).astype(F32), jnp.uint32)
    w = x.shape[1] // 2
    return (bits[:, :w] >> 16) | bits[:, w:]


def _unpack_bf16_pairs(p):
    lo = lax.bitcast_convert_type(p << 16, F32).astype(BF16)
    hi = lax.bitcast_convert_type(p & jnp.uint32(0xFFFF0000), F32).astype(BF16)
    return lo, hi


def _merge_body(*refs, n_prompt, top_k):
    prompt, sample, rest = refs[0:5], refs[5:10], refs[10:]
    i = pl.program_id(0)

    @pl.when(i < n_prompt)
    def _():
        _merge_block(*prompt, *rest, top_k=top_k)

    @pl.when(i >= n_prompt)
    def _():
        _merge_block(*sample, *rest, top_k=top_k)


def _merge_block(ya_ref, yr_ref, ga_ref, gb_ref, x_ref, wa_ref, wb_ref, wo_ref, nf_ref, wr_ref, br_ref,
                 h_ref, tok_ref, ids_ref, gt_ref, cnt_ref, *, top_k):
    mix = (jax.nn.sigmoid(ga_ref[...]) * _dot(ya_ref[...], wa_ref[...])
           + jax.nn.sigmoid(gb_ref[...]) * _dot(yr_ref[...], wb_ref[...]))
    h = x_ref[...] + _dot(mix.astype(BF16), wo_ref[...])
    h_ref[...] = h
    tok = _rms(h, nf_ref[...])
    tok_ref[...] = _pack_bf16_pairs(tok)
    logits = lax.dot_general(wr_ref[...], tok.astype(BF16), _NT, preferred_element_type=F32) + br_ref[...]
    n_exp, tm = logits.shape
    eidx = lax.broadcasted_iota(jnp.int32, (n_exp, tm), 0).astype(F32)
    ids, tops = [], []
    for _ in range(top_k):
        m = jnp.max(logits, axis=0, keepdims=True)
        sel = jnp.min(jnp.where(logits == m, eidx, float(n_exp)), axis=0, keepdims=True)
        ids.append(sel)
        tops.append(m)
        logits = jnp.where(eidx == sel, -jnp.inf, logits)
    ex = [jnp.exp(t - tops[0]) for t in tops]
    inv = 1.0 / functools.reduce(lambda a, b: a + b, ex)
    ids_ref[...] = jnp.concatenate(ids, axis=0).astype(jnp.int32)
    gt_ref[...] = jnp.concatenate([e * inv for e in ex], axis=0)
    cnt = jnp.zeros((n_exp, 1), F32)
    for sel in ids:
        cnt = cnt + jnp.sum(jnp.where(eidx == sel, 1.0, 0.0), axis=1, keepdims=True)
    cnt_ref[0] = cnt


def _local_positions(ids_ref, seg_col, n_exp):
    top_k, tb = ids_ref.shape
    ids = ids_ref[...].astype(F32)
    eidx = lax.broadcasted_iota(jnp.int32, (n_exp, tb), 0).astype(F32)
    r = lax.broadcasted_iota(jnp.int32, (tb, tb), 0)
    c = lax.broadcasted_iota(jnp.int32, (tb, tb), 1)
    before = jnp.where(r < c, 1.0, 0.0).astype(BF16)
    run = seg_col
    rows = []
    for k in range(top_k):
        onehot = jnp.where(ids[k:k + 1, :] == eidx, 1.0, 0.0)
        pos = _dot(onehot.astype(BF16), before) + run
        rows.append(jnp.sum(onehot * pos, axis=0, keepdims=True))
        run = run + jnp.sum(onehot, axis=1, keepdims=True)
    return jnp.concatenate(rows, axis=0)


def _row_windows(n, emit):
    n_big = n // SEG_BIG

    def big(j, _):
        emit(j * SEG_BIG, SEG_BIG)
        return 0

    lax.fori_loop(0, n_big, big, 0)
    off = n_big * SEG_BIG
    for bit in range(SEG_BIG.bit_length() - 2, SEG_ALIGN.bit_length() - 2, -1):
        size = 1 << bit
        take = (n >> bit) & 1

        @pl.when(take == 1)
        def _(off=off, size=size):
            emit(off, size)

        off = off + take * size


def _segment_copies(cnt_ref, seg_ref, row_ref, blk, n_exp, vmem, hbm, sem, *, to_hbm, wait):
    def per_expert(e, _):
        src0 = seg_ref[blk * n_exp + e]
        dst0 = row_ref[blk * n_exp + e]

        def window(off, size):
            v = vmem.at[pl.ds(pl.multiple_of(src0 + off, SEG_ALIGN), size), :]
            h = hbm.at[pl.ds(pl.multiple_of(dst0 + off, SEG_ALIGN), size), :]
            cp = pltpu.make_async_copy(v, h, sem) if to_hbm else pltpu.make_async_copy(h, v, sem)
            if wait:
                cp.wait()
            else:
                cp.start()

        _row_windows(cnt_ref[blk * n_exp + e], window)
        return 0

    lax.fori_loop(0, n_exp, per_expert, 0)


def _permutation(pos, n_rows):
    p = lax.broadcasted_iota(jnp.int32, (n_rows, pos.shape[1]), 0)
    out = jnp.where(pos[0:1, :] == p, 1.0, 0.0)
    for k in range(1, pos.shape[0]):
        out = out + jnp.where(pos[k:k + 1, :] == p, 1.0, 0.0)
    return out


def _dispatch_body(cnt_ref, seg_ref, row_ref, fill_ref, ids_ref, gt_ref, segc_ref, tok_ref, xs_out, pos_ref,
                   loc, zeros, sems, *, n_exp):
    blk = pl.program_id(0)
    last = pl.num_programs(0) - 1
    slot = blk % 2
    top_k, tb = ids_ref.shape
    words = tok_ref.shape[1]
    n_rows = loc.shape[1]

    def copies(b, s, wait):
        _segment_copies(cnt_ref, seg_ref, row_ref, b, n_exp, loc.at[s], xs_out, sems.at[s], to_hbm=True, wait=wait)

    @pl.when(blk == 0)
    def _():
        zeros[...] = jnp.zeros_like(zeros)

        def fill(wait):
            def gap(e, _):
                row = fill_ref[e]

                def window(off, size):
                    cp = pltpu.make_async_copy(zeros.at[pl.ds(0, size), :],
                                               xs_out.at[pl.ds(pl.multiple_of(row + off, SEG_ALIGN), size), :],
                                               sems.at[0])
                    if wait:
                        cp.wait()
                    else:
                        cp.start()

                _row_windows(fill_ref[n_exp + 1 + e], window)
                return 0

            lax.fori_loop(0, n_exp + 1, gap, 0)

        fill(False)
        fill(True)

    @pl.when(blk >= 2)
    def _():
        copies(blk - 2, slot, True)

    pos = _local_positions(ids_ref, segc_ref[0], n_exp).astype(jnp.int32)
    pos_ref[...] = pos
    perm = _permutation(pos, n_rows).astype(BF16)
    lo, hi = _unpack_bf16_pairs(tok_ref[...])
    lo_bits = lax.bitcast_convert_type(_dot(perm, lo), jnp.uint32)
    hi_bits = lax.bitcast_convert_type(_dot(perm, hi), jnp.uint32)
    buf = loc.at[slot]
    buf[:, 0:words] = (lo_bits >> 16) | hi_bits
    p = lax.broadcasted_iota(jnp.int32, (n_rows, tb), 0)
    gates = gt_ref[...]
    spread = jnp.where(pos[0:1, :] == p, gates[0:1, :], 0.0)
    for k in range(1, top_k):
        spread = spread + jnp.where(pos[k:k + 1, :] == p, gates[k:k + 1, :], 0.0)
    gate_row = jnp.sum(spread, axis=1, keepdims=True)
    buf[:, words:] = lax.bitcast_convert_type(jnp.broadcast_to(gate_row, (n_rows, LANES)), jnp.uint32)
    copies(blk, slot, False)

    @pl.when(blk == last)
    def _():
        @pl.when(blk >= 1)
        def _():
            copies(blk - 1, 1 - slot, True)

        copies(blk, slot, True)


def _expert_body(be_ref, nu_ref, x_ref, wgu_ref, bgu_ref, wd_ref, bd_ref, o_ref, wgu_b, wd_b, *, d_ff):
    i = pl.program_id(0)
    used = i < nu_ref[0]

    @pl.when(jnp.logical_and(used, jnp.logical_or(i == 0, be_ref[i] != be_ref[jnp.maximum(i - 1, 0)])))
    def _():
        wgu_b[...] = wgu_ref[0].astype(BF16)
        wd_b[...] = wd_ref[0].astype(BF16)

    @pl.when(used)
    def _():
        x = x_ref[...]
        half = x.shape[1] - LANES
        lo, hi = _unpack_bf16_pairs(x[:, :half])
        gate = lax.bitcast_convert_type(x[:, half:half + 1], F32)
        h = _dot(lo, wgu_b[:half, :]) + _dot(hi, wgu_b[half:, :]) + bgu_ref[0]
        g = jnp.minimum(h[:, :d_ff], SWIGLU_LIMIT)
        u = jnp.clip(h[:, d_ff:], -SWIGLU_LIMIT, SWIGLU_LIMIT)
        act = (u + 1.0) * (g * jax.nn.sigmoid(SWIGLU_ALPHA * g))
        out = (_dot(act.astype(BF16), wd_b[...]) + bd_ref[0]) * gate
        hi_part, lo_part = _split_bf16(out)
        o_ref[...] = (lax.bitcast_convert_type(hi_part.astype(F32), jnp.uint32)
                      | (lax.bitcast_convert_type(lo_part.astype(F32), jnp.uint32) >> 16))

    @pl.when(jnp.logical_not(used))
    def _():
        o_ref[...] = jnp.zeros_like(o_ref)


def _combine_body(cnt_ref, seg_ref, row_ref, pos_ref, h_ref, nf_ref, os_hbm, y_ref, loc, sems, *, n_exp, blk_off):
    i = pl.program_id(0)
    slot = i % 2
    n_rows = loc.shape[1]
    n_pairs = pos_ref.shape[0] * pos_ref.shape[1]

    def fetch(step, s, wait):
        _segment_copies(cnt_ref, seg_ref, row_ref, step + blk_off, n_exp, loc.at[s], os_hbm, sems.at[s],
                        to_hbm=False, wait=wait)

    def prepare(step, s):
        loc[s, n_pairs:, :] = jnp.zeros((n_rows - n_pairs, loc.shape[2]), loc.dtype)
        fetch(step, s, False)

    @pl.when(i == 0)
    def _():
        prepare(0, 0)

    @pl.when(i + 1 < pl.num_programs(0))
    def _():
        prepare(i + 1, 1 - slot)

    perm = _permutation(pos_ref[...], n_rows).astype(BF16)
    fetch(i, slot, True)
    w = loc[slot]
    hi_part = lax.bitcast_convert_type(w & jnp.uint32(0xFFFF0000), F32).astype(BF16)
    lo_part = lax.bitcast_convert_type(w << 16, F32).astype(BF16)
    f = (lax.dot_general(perm, hi_part, _TN, preferred_element_type=F32)
         + lax.dot_general(perm, lo_part, _TN, preferred_element_type=F32))
    y_ref[...] = _rms(h_ref[...] + f, nf_ref[...])


def _largest_pow2_divisor(n, cap):
    t = cap
    while n % t:
        t //= 2
    return t


def _rotation_tables(pos, half):
    inv_freq = 1.0 / (ROPE_BASE ** jnp.linspace(0.0, 1.0, half, dtype=F32))
    ang = pos.astype(F32)[:, None] * inv_freq[None, :]
    cos, sin = jnp.cos(ang), jnp.sin(ang)
    return jnp.concatenate([cos, cos], axis=1), jnp.concatenate([-sin, sin], axis=1)


def _decay_tables(log_g, c):
    idx = jnp.arange(c, dtype=F32)
    lg = log_g[:, None]
    rel = idx[:, None] - idx[None, :]
    dec = jnp.where(rel >= 0, jnp.exp(lg[:, :, None] * jnp.maximum(rel, 0.0)), 0.0)
    rowdec = jnp.exp(lg * (idx + 1.0))[:, :, None]
    coldec = jnp.exp(lg * (c - 1.0 - idx))[:, :, None]
    gc = jnp.exp(lg * c)[:, :, None]
    return dec, rowdec, coldec, gc


def kernel(x_prompt, x_sample, cache_sb_k, cache_sb_v, state_ret, meta_tokens, norm_mix, w_in, w_branch_sb, w_branch_ret, w_out, norm_ffn, w_router, b_router, w_gate_up, b_gate_up, w_down, b_down, norm_final):
    B, SEQ, D = x_prompt.shape
    Bs, Ts, _ = x_sample.shape
    depth, _, past, h_sb, dh = cache_sb_k.shape
    _, _, h_ret, dk, dv = state_ret.shape
    n_meta = meta_tokens.shape[0]
    n_exp = w_router.shape[-1]
    d_ff = w_down.shape[-2]
    sbw, rqk, rv = h_sb * dh, h_ret * dk, h_ret * dv
    assert depth == 1, "one layer: the meta-token rows feed nothing after the mixers"
    assert dk == LANES and 2 * dh == LANES and h_sb * Ts == LANES and n_meta % 16 == 0

    TQ = _largest_pow2_divisor(SEQ, 512)
    assert TQ >= 2 * LANES and n_meta <= LANES
    t_p, t_s = B * SEQ, Bs * Ts
    t_moe = t_p + t_s
    TM = _largest_pow2_divisor(t_moe, 512)
    assert TM >= 16 and SEQ % TM == 0 and t_p % TM == 0 and t_moe % n_meta == 0 and t_p % Ts == 0
    n_p, n_moe = t_p // TM, t_moe // TM
    t_x = t_s + n_meta
    meta_blk = t_s // n_meta

    x_p = x_prompt.reshape(t_p, D)
    x_x = jnp.concatenate([x_sample.reshape(t_s, D), meta_tokens.astype(F32)], axis=0)

    sizes = (sbw,) * 3 + (rqk,) * 2 + (rv,) * 2 + (D,) * 2
    offs = tuple(sum(sizes[:i]) for i in range(len(sizes) + 1))
    n_sq = SEQ // TM
    row = lambda w: pl.BlockSpec((TM, w), lambda i: (i, 0))
    out_dt = (BF16, F32, F32, F32, F32, BF16, F32, F32, F32)
    w_in_b, g_mix = w_in[0].astype(BF16), norm_mix[0][None, :]

    def inproj(x, pos, tm, tab_period):
        rows = x.shape[0]
        blk = lambda w: pl.BlockSpec((tm, w), lambda i: (i, 0))
        tab = pl.BlockSpec((tm, dk), lambda i: (i % tab_period, 0))
        return pl.pallas_call(
            functools.partial(_inproj_body, offs=offs, q_scale=dh ** -0.5, k_scale=dk ** -0.5, n_ret=h_ret, dk=dk),
            grid=(rows // tm,),
            in_specs=[blk(D), pl.BlockSpec((1, D), lambda i: (0, 0)),
                      pl.BlockSpec((D, offs[-1]), lambda i: (0, 0), pipeline_mode=pl.Buffered(1)), tab, tab],
            out_specs=[blk(w) for w in sizes],
            out_shape=[jax.ShapeDtypeStruct((rows, w), dt) for w, dt in zip(sizes, out_dt)],
            compiler_params=_params("parallel"), name="inproj",
        )(x, g_mix, w_in_b, *_rotation_tables(pos, dk // 2))

    qa, ka, va, qr, kr, vr, gr, ga, gb = inproj(x_p, jnp.arange(SEQ), TM, n_sq)
    x_pos = jnp.concatenate([jnp.tile(past + jnp.arange(Ts), Bs), jnp.arange(n_meta) - n_meta])
    qa_x, ka_x, va_x, qr_x, kr_x, vr_x, gr_x, ga_x, gb_x = inproj(x_x, x_pos, t_x, 1)

    n_q = SEQ // TQ
    n_pair = sbw // LANES
    ya_p = pl.pallas_call(
        functools.partial(_sb_prompt_body, tq=TQ, n_meta=n_meta, dh=dh),
        grid=(B, n_pair, n_q),
        in_specs=[pl.BlockSpec((TQ, LANES), lambda b, p, q: (b * n_q + q, p)),
                  pl.BlockSpec((SEQ, LANES), lambda b, p, q: (b, p)),
                  pl.BlockSpec((SEQ, LANES), lambda b, p, q: (b, p)),
                  pl.BlockSpec((n_meta, LANES), lambda b, p, q: (meta_blk, p)),
                  pl.BlockSpec((n_meta, LANES), lambda b, p, q: (meta_blk, p))],
        out_specs=pl.BlockSpec((TQ, LANES), lambda b, p, q: (b * n_q + q, p)),
        out_shape=jax.ShapeDtypeStruct((t_p, sbw), BF16),
        scratch_shapes=[pltpu.VMEM((LANES + SEQ, LANES), BF16), pltpu.VMEM((LANES + SEQ, LANES), BF16),
                        pltpu.VMEM((2 * TQ, 1), F32), pltpu.VMEM((2 * TQ, LANES), F32)],
        compiler_params=_params("parallel", "parallel", "arbitrary"), name="sb_prompt",
    )(qa, ka, va, ka_x, va_x)

    KC = _largest_pow2_divisor(past, 1024)
    SUB = min(KC, 256)
    n_kc = past // KC
    smp = lambda w: pl.BlockSpec((Ts, w), lambda b, j: (b, 0))
    met = lambda w: pl.BlockSpec((n_meta, w), lambda b, j: (meta_blk, 0))
    cache = pl.BlockSpec((1, KC, sbw), lambda b, j: (b, n_kc - 1 - j, 0))
    ya_s = pl.pallas_call(
        functools.partial(_sb_sample_body, ts=Ts, n_meta=n_meta, dh=dh, sub=SUB),
        grid=(Bs, n_kc),
        in_specs=[smp(sbw), smp(sbw), smp(sbw), met(sbw), met(sbw), cache, cache],
        out_specs=pl.BlockSpec((Ts, sbw), lambda b, j: (b, 0)),
        out_shape=jax.ShapeDtypeStruct((t_s, sbw), BF16),
        scratch_shapes=[pltpu.VMEM((LANES, sbw), BF16), pltpu.VMEM((1, LANES), F32), pltpu.VMEM((LANES, sbw), F32)],
        compiler_params=_params("parallel", "arbitrary"), name="sb_sample",
    )(qa_x, ka_x, va_x, ka_x, va_x, cache_sb_k[0].reshape(Bs, past, sbw), cache_sb_v[0].reshape(Bs, past, sbw))

    log_g = jnp.log(1.0 - 2.0 ** (-5.0 - jnp.arange(h_ret, dtype=F32)))
    RC = _largest_pow2_divisor(SEQ, 256)
    dec, rd, cd, gc = _decay_tables(log_g, RC)
    _, _, mcd, _ = _decay_tables(log_g, n_meta)
    per_b = lambda w: pl.BlockSpec((SEQ, w), lambda b, h: (b, h))
    met_h = lambda w: pl.BlockSpec((n_meta, w), lambda b, h: (meta_blk, h))
    tab = lambda a: pl.BlockSpec((1,) + a.shape[1:], lambda b, h: (h, 0, 0))
    state_spec = pl.BlockSpec((1, 1, dk, dv), lambda b, h: (b, h, 0, 0))
    yr_p, s_p = pl.pallas_call(
        functools.partial(_ret_prompt_body, chunk=RC),
        grid=(B, h_ret),
        in_specs=[per_b(dk), per_b(dk), per_b(dv), per_b(dv), met_h(dk), met_h(dv),
                  tab(dec), tab(rd), tab(cd), tab(gc), tab(mcd)],
        out_specs=[per_b(dv), state_spec],
        out_shape=[jax.ShapeDtypeStruct((t_p, rv), BF16), jax.ShapeDtypeStruct((B, h_ret, dk, dv), F32)],
        compiler_params=_params("parallel", "parallel"), name="ret_prompt",
    )(qr, kr, vr, gr, kr_x, vr_x, dec, rd, cd, gc, mcd)

    dec_s, rd_s, cd_s, gc_s = _decay_tables(log_g, Ts)
    smp_h = lambda w: pl.BlockSpec((Ts, w), lambda b, h: (b, h))
    yr_s, s_s = pl.pallas_call(
        _ret_sample_body,
        grid=(Bs, h_ret),
        in_specs=[smp_h(dk), smp_h(dk), smp_h(dv), smp_h(dv), state_spec,
                  tab(dec_s), tab(rd_s), tab(cd_s), tab(gc_s)],
        out_specs=[smp_h(dv), state_spec],
        out_shape=[jax.ShapeDtypeStruct((t_s, rv), BF16), jax.ShapeDtypeStruct((Bs, h_ret, dk, dv), F32)],
        compiler_params=_params("parallel", "parallel"), name="ret_sample",
    )(qr_x, kr_x, vr_x, gr_x, state_ret[0], dec_s, rd_s, cd_s, gc_s)

    const = lambda a: pl.BlockSpec(a.shape, lambda i: (0,) * a.ndim)
    prm = lambda w: pl.BlockSpec((TM, w), lambda i: (jnp.minimum(i, n_p - 1), 0))
    smp_m = lambda w: pl.BlockSpec((TM, w), lambda i: (jnp.maximum(i - n_p, 0), 0))
    lane_major = pl.BlockSpec((TOP_K, TM), lambda i: (0, i))
    wa, wb, wo = w_branch_sb[0].astype(BF16), w_branch_ret[0].astype(BF16), w_out[0].astype(BF16)
    nf, wr_t, br = norm_ffn[0][None, :], w_router[0].T.astype(BF16), b_router[0][:, None]
    h1, tok, ids, gates, cnt = pl.pallas_call(
        functools.partial(_merge_body, n_prompt=n_p, top_k=TOP_K),
        grid=(n_moe,),
        in_specs=[prm(sbw), prm(rv), prm(D), prm(D), prm(D), smp_m(sbw), smp_m(rv), smp_m(D), smp_m(D), smp_m(D),
                  const(wa), const(wb), const(wo), const(nf), const(wr_t), const(br)],
        out_specs=[row(D), row(D // 2), lane_major, lane_major, pl.BlockSpec((1, n_exp, 1), lambda i: (i, 0, 0))],
        out_shape=[jax.ShapeDtypeStruct((t_moe, D), F32), jax.ShapeDtypeStruct((t_moe, D // 2), jnp.uint32),
                   jax.ShapeDtypeStruct((TOP_K, t_moe), jnp.int32), jax.ShapeDtypeStruct((TOP_K, t_moe), F32),
                   jax.ShapeDtypeStruct((n_moe, n_exp, 1), F32)],
        compiler_params=_params("parallel"), name="merge_router",
    )(ya_p, yr_p, ga, gb, x_p, ya_s, yr_s, ga_x, gb_x, x_x, wa, wb, wo, nf, wr_t, br)

    BM = TM
    cnt = cnt[:, :, 0].astype(jnp.int32)
    cnt = (cnt + SEG_ALIGN - 1) // SEG_ALIGN * SEG_ALIGN
    padded = (jnp.sum(cnt, axis=0) + BM - 1) // BM * BM
    pad_end = jnp.cumsum(padded)
    seg_row = (pad_end - padded)[None, :] + jnp.cumsum(cnt, axis=0) - cnt
    seg_loc = jnp.cumsum(cnt, axis=1) - cnt
    n_blk = -(-(t_moe * TOP_K + (SEG_ALIGN - 1) * n_moe * n_exp) // BM) + n_exp
    block_e = jnp.minimum(jnp.sum(pad_end[None, :] <= (jnp.arange(n_blk) * BM)[:, None], axis=1), n_exp - 1).astype(jnp.int32)
    n_used = (pad_end[-1:] // BM).astype(jnp.int32)
    seg_tabs = (cnt.reshape(-1), seg_loc.reshape(-1).astype(jnp.int32), seg_row.reshape(-1).astype(jnp.int32))
    grp_end = pad_end - padded + jnp.sum(cnt, axis=0)
    fill_rows = jnp.concatenate([grp_end, pad_end[-1:], pad_end - grp_end,
                                 (n_blk + 1) * BM - pad_end[-1:]]).astype(jnp.int32)

    XW = D // 2 + LANES
    n_loc = -(-(TOP_K * TM + SEG_ALIGN * n_exp) // LANES) * LANES
    xs, pos = pl.pallas_call(
        functools.partial(_dispatch_body, n_exp=n_exp),
        grid_spec=pltpu.PrefetchScalarGridSpec(
            num_scalar_prefetch=4, grid=(n_moe,),
            in_specs=[pl.BlockSpec((TOP_K, TM), lambda i, *_: (0, i)), pl.BlockSpec((TOP_K, TM), lambda i, *_: (0, i)),
                      pl.BlockSpec((1, n_exp, 1), lambda i, *_: (i, 0, 0)),
                      pl.BlockSpec((TM, D // 2), lambda i, *_: (i, 0))],
            out_specs=[pl.BlockSpec(memory_space=pl.ANY), pl.BlockSpec((TOP_K, TM), lambda i, *_: (0, i))],
            scratch_shapes=[pltpu.VMEM((2, n_loc, XW), jnp.uint32), pltpu.VMEM((SEG_BIG, XW), jnp.uint32),
                            pltpu.SemaphoreType.DMA((2,))]),
        out_shape=[jax.ShapeDtypeStruct(((n_blk + 1) * BM, XW), jnp.uint32),
                   jax.ShapeDtypeStruct((TOP_K, t_moe), jnp.int32)],
        compiler_params=_params("arbitrary"), name="moe_dispatch",
    )(*seg_tabs, fill_rows, ids, gates, seg_loc.astype(F32)[:, :, None], tok)

    blk = lambda i, be, nu: jnp.minimum(i, nu[0] - 1)
    exp_w = lambda a: pl.BlockSpec((1,) + a.shape[1:], lambda i, be, nu: (be[blk(i, be, nu)],) + (0,) * (a.ndim - 1))
    wgu, bgu = w_gate_up[0], b_gate_up[0][:, None, :]
    wd, bd = w_down[0], b_down[0][:, None, :]
    os_ = pl.pallas_call(
        functools.partial(_expert_body, d_ff=d_ff),
        grid_spec=pltpu.PrefetchScalarGridSpec(
            num_scalar_prefetch=2, grid=(n_blk,),
            in_specs=[pl.BlockSpec((BM, XW), lambda i, be, nu: (blk(i, be, nu), 0)),
                      exp_w(wgu), exp_w(bgu), exp_w(wd), exp_w(bd)],
            out_specs=pl.BlockSpec((BM, D), lambda i, be, nu: (i, 0)),
            scratch_shapes=[pltpu.VMEM(wgu.shape[1:], BF16), pltpu.VMEM(wd.shape[1:], BF16)]),
        out_shape=jax.ShapeDtypeStruct((n_blk * BM, D), jnp.uint32),
        compiler_params=_params("arbitrary"), name="moe_experts",
    )(block_e, n_used, xs, wgu, bgu, wd, bd)

    def combine(off, n):
        return pl.pallas_call(
            functools.partial(_combine_body, n_exp=n_exp, blk_off=off),
            grid_spec=pltpu.PrefetchScalarGridSpec(
                num_scalar_prefetch=3, grid=(n,),
                in_specs=[pl.BlockSpec((TOP_K, TM), lambda i, *_: (0, i + off)),
                          pl.BlockSpec((TM, D), lambda i, *_: (i + off, 0)),
                          pl.BlockSpec((1, D), lambda i, *_: (0, 0)), pl.BlockSpec(memory_space=pl.ANY)],
                out_specs=pl.BlockSpec((TM, D), lambda i, *_: (i, 0)),
                scratch_shapes=[pltpu.VMEM((2, n_loc, D), jnp.uint32), pltpu.SemaphoreType.DMA((2,))]),
            out_shape=jax.ShapeDtypeStruct((n * TM, D), F32),
            compiler_params=_params("arbitrary"), name="moe_combine",
        )(*seg_tabs, pos, h1, norm_final[None, :], os_)

    y_prompt = combine(0, n_p).reshape(B, SEQ, D)
    y_sample = combine(n_p, n_moe - n_p).reshape(Bs, Ts, D)

    def with_meta(a, a_x):
        meta_rows = jnp.broadcast_to(a_x[t_s:].reshape(1, n_meta, h_sb, dh), (B, n_meta, h_sb, dh))
        return jnp.concatenate([meta_rows, a.reshape(B, SEQ, h_sb, dh)], axis=1)[None]

    return (y_prompt, y_sample, with_meta(ka, ka_x), with_meta(va, va_x), s_p[None],
            ka_x[:t_s].reshape(1, Bs, Ts, h_sb, dh), va_x[:t_s].reshape(1, Bs, Ts, h_sb, dh), s_s[None])
```

```python
import functools

import jax
import jax.numpy as jnp
from jax import lax
from jax.experimental import pallas as pl
from jax.experimental.pallas import tpu as pltpu

F32 = jnp.float32
BF16 = jnp.bfloat16

EPS = 1e-6
ROPE_BASE = 10000.0
TOP_K = 4
SWIGLU_LIMIT = 7.0
SWIGLU_ALPHA = 1.702
LOG2E = 1.4426950408889634

LANES = 128
SEG_ALIGN = 8
SEG_BIG = 256
VMEM_LIMIT = 56 << 20

_NT = (((1,), (1,)), ((), ()))
_TN = (((0,), (0,)), ((), ()))


def _params(*sem, **kw):
    return pltpu.CompilerParams(dimension_semantics=sem, vmem_limit_bytes=VMEM_LIMIT, **kw)


def _dot(a, b):
    return jnp.dot(a, b, preferred_element_type=F32)


def _rms(x, g):
    return x * lax.rsqrt(jnp.mean(x * x, axis=-1, keepdims=True) + EPS) * g


def _inproj_body(x_ref, g_ref, w_ref, cs_ref, sn_ref, qa_ref, ka_ref, va_ref, qr_ref, kr_ref,
                 vr_ref, gr_ref, ga_ref, gb_ref, *, offs, q_scale, k_scale, n_ret, dk):
    xn = _rms(x_ref[...], g_ref[...]).astype(BF16)

    def proj(i):
        return _dot(xn, w_ref[:, offs[i]:offs[i + 1]])

    qa_ref[...] = (proj(0) * q_scale).astype(BF16)
    ka_ref[...] = proj(1)
    va_ref[...] = proj(2)
    cs = cs_ref[...]
    sn = sn_ref[...]
    for ref, i, scale in ((qr_ref, 3, None), (kr_ref, 4, k_scale)):
        t = proj(i)
        for h in range(n_ret):
            th = t[:, h * dk:(h + 1) * dk]
            r = th * cs + pltpu.roll(th, dk // 2, axis=1) * sn
            ref[:, h * dk:(h + 1) * dk] = r if scale is None else r * scale
    vr_ref[...] = proj(5).astype(BF16)
    gr_ref[...] = proj(6)
    ga_ref[...] = proj(7)
    gb_ref[...] = proj(8)


def _sb_scores(z):
    log_beta = jnp.minimum(z, 0.0) - jnp.log(1.0 + jnp.exp(-jnp.abs(z)))
    return log_beta, log_beta - z


def _split_bf16(x):
    hi = x.astype(BF16)
    return hi, (x - hi.astype(F32)).astype(BF16)


def _suffix_ones(n):
    r = lax.broadcasted_iota(jnp.int32, (2 * n, n), 0) & (n - 1)
    c = lax.broadcasted_iota(jnp.int32, (2 * n, n), 1)
    return jnp.where(r >= c, 1.0, 0.0).astype(BF16)


def _sb_logits(qs, kblk):
    return lax.dot_general(qs, kblk, _NT, preferred_element_type=F32) * LOG2E


def _sb_suffix(z2, suffix2, mask):
    neg_abs = lax.bitcast_convert_type(lax.bitcast_convert_type(z2, jnp.uint32) | jnp.uint32(0x80000000), F32)
    drop = jnp.maximum(z2, 0.0) + jnp.log2(1.0 + jnp.exp2(neg_abs))
    if mask is not None:
        drop = jnp.where(mask, drop, 0.0)
    hi, lo = _split_bf16(drop)
    return _dot(jnp.concatenate([hi, lo], axis=1), suffix2)


def _sb_weights(z2, incl, carry, mask):
    w = jnp.exp2(z2 - (incl if carry is None else incl + carry))
    if mask is not None:
        w = jnp.where(mask, w, 0.0)
    return w.astype(BF16)


def _sb_block(qs, kblk, vblk, suffix2, mask, carry, acc):
    z2 = _sb_logits(qs, kblk)
    incl = _sb_suffix(z2, suffix2, mask)
    return carry + incl[:, 0:1], acc + _dot(_sb_weights(z2, incl, carry, mask), vblk)


def _sb_prompt_body(q_ref, k_ref, v_ref, mk_ref, mv_ref, o_ref, kb, vb, carry_ref, acc_ref, *, tq, n_meta, dh):
    qi = pl.program_id(2)
    tk = tq // 2
    mpad = LANES - n_meta
    n_heads = LANES // dh

    @pl.when(qi == 0)
    def _():
        kb[0:mpad, :] = jnp.zeros((mpad, LANES), BF16)
        vb[0:mpad, :] = jnp.zeros((mpad, LANES), BF16)
        kb[mpad:LANES, :] = mk_ref[...].astype(BF16)
        vb[mpad:LANES, :] = mv_ref[...].astype(BF16)
        kb[LANES:, :] = k_ref[...].astype(BF16)
        vb[LANES:, :] = v_ref[...].astype(BF16)

    q = q_ref[...]
    lane = lax.broadcasted_iota(jnp.int32, (tk, LANES), 1)
    parts = []
    for half in range(2):
        qh = q[half * tk:(half + 1) * tk]
        for head in range(n_heads):
            parts.append(jnp.where(lane // dh == head, qh, jnp.zeros_like(qh)))
    qs = jnp.concatenate(parts, axis=0)
    m_half = n_heads * tk
    suffix2 = _suffix_ones(tk)
    base = pl.multiple_of(LANES + qi * tq, LANES)

    r = lax.broadcasted_iota(jnp.int32, (m_half, tk), 0) & (tk - 1)
    c = lax.broadcasted_iota(jnp.int32, (m_half, tk), 1)
    mask_b = c < r
    r = lax.broadcasted_iota(jnp.int32, (2 * m_half, tk), 0)
    c = lax.broadcasted_iota(jnp.int32, (2 * m_half, tk), 1)
    mask_a = (r >= m_half) | (c < (r & (tk - 1)))
    z_b = _sb_logits(qs[m_half:], kb[pl.ds(base + tk, tk), :])
    z_a = _sb_logits(qs, kb[pl.ds(base, tk), :])
    incl_b = _sb_suffix(z_b, suffix2, mask_b)
    incl_a = _sb_suffix(z_a, suffix2, mask_a)
    acc_b = _dot(_sb_weights(z_b, incl_b, None, mask_b), vb[pl.ds(base + tk, tk), :])
    carry = jnp.concatenate([jnp.zeros((m_half, 1), F32), incl_b[:, 0:1]], axis=0)
    acc = jnp.concatenate([jnp.zeros((m_half, LANES), F32), acc_b], axis=0)
    acc_ref[...] = acc + _dot(_sb_weights(z_a, incl_a, carry, mask_a), vb[pl.ds(base, tk), :])
    carry_ref[...] = carry + incl_a[:, 0:1]

    def left(it, _):
        offs = [pl.multiple_of(base - (2 * it + step) * tk, LANES) for step in (1, 2)]
        z2 = [_sb_logits(qs, kb[pl.ds(o, tk), :]) for o in offs]
        incl = [_sb_suffix(z, suffix2, None) for z in z2]
        carry, acc = carry_ref[...], acc_ref[...]
        for z, inc, o in zip(z2, incl, offs):
            acc = acc + _dot(_sb_weights(z, inc, carry, None), vb[pl.ds(o, tk), :])
            carry = carry + inc[:, 0:1]
        carry_ref[...] = carry
        acc_ref[...] = acc
        return 0

    lax.fori_loop(0, qi, left, 0)
    c = lax.broadcasted_iota(jnp.int32, (2 * m_half, LANES), 1)
    _, acc = _sb_block(qs, kb[0:LANES, :], vb[0:LANES, :], _suffix_ones(LANES), c >= mpad,
                       carry_ref[...], acc_ref[...])
    for half in range(2):
        for head in range(n_heads):
            rows = (half * n_heads + head) * tk
            o_ref[half * tk:(half + 1) * tk, head * dh:(head + 1) * dh] = (
                acc[rows:rows + tk, head * dh:(head + 1) * dh].astype(o_ref.dtype))


def _sb_sample_body(q_ref, nk_ref, nv_ref, mk_ref, mv_ref, ck_ref, cv_ref, o_ref,
                    qbd, carry_ref, acc_ref, *, ts, n_meta, dh, sub):
    j = pl.program_id(1)
    width = q_ref.shape[1]
    n_heads = width // dh

    def head_mask(shape):
        r = lax.broadcasted_iota(jnp.int32, shape, 0)
        c = lax.broadcasted_iota(jnp.int32, shape, 1)
        return r // ts == c // dh

    def sweep(k, v, mask, carry, acc):
        n = k.shape[0]
        z = lax.dot_general(k.astype(BF16), qbd[...], _NT, preferred_element_type=F32)
        log_beta, log_keep = _sb_scores(z)
        if mask is not None:
            log_keep = jnp.where(mask, log_keep, 0.0)
        r = lax.broadcasted_iota(jnp.int32, (n, n), 0)
        c = lax.broadcasted_iota(jnp.int32, (n, n), 1)
        suffix = jnp.where(c >= r, 1.0, 0.0).astype(BF16)
        hi, lo = _split_bf16(log_keep)
        incl = _dot(suffix, hi) + _dot(suffix, lo)
        w = jnp.exp(log_beta + (carry + incl - log_keep))
        if mask is not None:
            w = jnp.where(mask, w, 0.0)
        acc = acc + lax.dot_general(w.astype(BF16), v.astype(BF16), _TN, preferred_element_type=F32)
        return carry + incl[0:1, :], acc

    @pl.when(j == 0)
    def _():
        qt = jnp.concatenate([q_ref[...]] * n_heads, axis=0)
        qbd[...] = jnp.where(head_mask((n_heads * ts, width)), qt, jnp.zeros_like(qt))
        key = lax.broadcasted_iota(jnp.int32, (ts, LANES), 0)
        qry = lax.broadcasted_iota(jnp.int32, (ts, LANES), 1) % ts
        carry, acc = sweep(nk_ref[...], nv_ref[...], key < qry,
                           jnp.zeros((1, LANES), F32), jnp.zeros((LANES, width), F32))
        carry_ref[...] = carry
        acc_ref[...] = acc

    carry = carry_ref[...]
    acc = acc_ref[...]
    chunk = ck_ref.shape[1]
    for s in range(chunk // sub - 1, -1, -1):
        carry, acc = sweep(ck_ref[0, s * sub:(s + 1) * sub, :], cv_ref[0, s * sub:(s + 1) * sub, :],
                           None, carry, acc)
    carry_ref[...] = carry
    acc_ref[...] = acc

    @pl.when(j == pl.num_programs(1) - 1)
    def _():
        _, acc = sweep(mk_ref[...], mv_ref[...], None, carry_ref[...], acc_ref[...])
        own = jnp.where(head_mask((n_heads * ts, width)), acc, 0.0)
        o_ref[...] = jnp.sum(own.reshape(n_heads, ts, width), axis=0).astype(o_ref.dtype)


def _ret_chunk(q, k, v, state, dec, rowdec, coldec, gc):
    qb = q.astype(BF16)
    inner = lax.dot_general(qb, k.astype(BF16), _NT, preferred_element_type=F32) * dec
    o = _dot(inner.astype(BF16), v) + _dot(qb, state.astype(BF16)) * rowdec
    kw = (k * coldec).astype(BF16)
    return gc * state + lax.dot_general(kw, v, _TN, preferred_element_type=F32), o


def _ret_gate(o, g):
    on = o * lax.rsqrt(jnp.mean(o * o, axis=-1, keepdims=True) + EPS)
    return (g * jax.nn.sigmoid(g)) * on


def _ret_prompt_body(q_ref, k_ref, v_ref, g_ref, mk_ref, mv_ref, dec_ref, rd_ref, cd_ref, gc_ref,
                     mcd_ref, y_ref, s_ref, *, chunk):
    kw = (mk_ref[...] * mcd_ref[0]).astype(BF16)
    state = lax.dot_general(kw, mv_ref[...], _TN, preferred_element_type=F32)
    dec, rd, cd, gc = dec_ref[0], rd_ref[0], cd_ref[0], gc_ref[0]

    def step(c, state):
        r = pl.ds(pl.multiple_of(c * chunk, chunk), chunk)
        state, o = _ret_chunk(q_ref[r, :], k_ref[r, :], v_ref[r, :], state, dec, rd, cd, gc)
        y_ref[r, :] = _ret_gate(o, g_ref[r, :]).astype(y_ref.dtype)
        return state

    s_ref[0, 0] = lax.fori_loop(0, q_ref.shape[0] // chunk, step, state)


def _ret_sample_body(q_ref, k_ref, v_ref, g_ref, s0_ref, dec_ref, rd_ref, cd_ref, gc_ref,
                     y_ref, s_ref):
    state, o = _ret_chunk(q_ref[...], k_ref[...], v_ref[...], s0_ref[0, 0], dec_ref[0], rd_ref[0],
                          cd_ref[0], gc_ref[0])
    y_ref[...] = _ret_gate(o, g_ref[...]).astype(y_ref.dtype)
    s_ref[0, 0] = state


def _pack_bf16_pairs(x):
    bits = lax.bitcast_convert_type(x.astype(BF16).astype(F32), jnp.uint32)
    w = x.shape[1] // 2
    return (bits[:, :w] >> 16) | bits[:, w:]


def _unpack_bf16_pairs(p):
    lo = lax.bitcast_convert_type(p << 16, F32).astype(BF16)
    hi = lax.bitcast_convert_type(p & jnp.uint32(0xFFFF0000), F32).astype(BF16)
    return lo, hi


def _merge_body(*refs, n_prompt, top_k):
    prompt, sample, rest = refs[0:5], refs[5:10], refs[10:]
    i = pl.program_id(0)

    @pl.when(i < n_prompt)
    def _():
        _merge_block(*prompt, *rest, top_k=top_k)

    @pl.when(i >= n_prompt)
    def _():
        _merge_block(*sample, *rest, top_k=top_k)


def _merge_block(ya_ref, yr_ref, ga_ref, gb_ref, x_ref, wa_ref, wb_ref, wo_ref, nf_ref, wr_ref, br_ref,
                 h_ref, tok_ref, ids_ref, gt_ref, cnt_ref, *, top_k):
    mix = (jax.nn.sigmoid(ga_ref[...]) * _dot(ya_ref[...], wa_ref[...])
           + jax.nn.sigmoid(gb_ref[...]) * _dot(yr_ref[...], wb_ref[...]))
    h = x_ref[...] + _dot(mix.astype(BF16), wo_ref[...])
    h_ref[...] = h
    tok = _rms(h, nf_ref[...])
    tok_ref[...] = _pack_bf16_pairs(tok)
    logits = lax.dot_general(wr_ref[...], tok.astype(BF16), _NT, preferred_element_type=F32) + br_ref[...]
    n_exp, tm = logits.shape
    eidx = lax.broadcasted_iota(jnp.int32, (n_exp, tm), 0).astype(F32)
    ids, tops = [], []
    for _ in range(top_k):
        m = jnp.max(logits, axis=0, keepdims=True)
        sel = jnp.min(jnp.where(logits == m, eidx, float(n_exp)), axis=0, keepdims=True)
        ids.append(sel)
        tops.append(m)
        logits = jnp.where(eidx == sel, -jnp.inf, logits)
    ex = [jnp.exp(t - tops[0]) for t in tops]
    inv = 1.0 / functools.reduce(lambda a, b: a + b, ex)
    ids_ref[...] = jnp.concatenate(ids, axis=0).astype(jnp.int32)
    gt_ref[...] = jnp.concatenate([e * inv for e in ex], axis=0)
    cnt = jnp.zeros((n_exp, 1), F32)
    for sel in ids:
        cnt = cnt + jnp.sum(jnp.where(eidx == sel, 1.0, 0.0), axis=1, keepdims=True)
    cnt_ref[0] = cnt


def _local_positions(ids_ref, seg_col, n_exp):
    top_k, tb = ids_ref.shape
    ids = ids_ref[...].astype(F32)
    eidx = lax.broadcasted_iota(jnp.int32, (n_exp, tb), 0).astype(F32)
    r = lax.broadcasted_iota(jnp.int32, (tb, tb), 0)
    c = lax.broadcasted_iota(jnp.int32, (tb, tb), 1)
    before = jnp.where(r < c, 1.0, 0.0).astype(BF16)
    run = seg_col
    rows = []
    for k in range(top_k):
        onehot = jnp.where(ids[k:k + 1, :] == eidx, 1.0, 0.0)
        pos = _dot(onehot.astype(BF16), before) + run
        rows.append(jnp.sum(onehot * pos, axis=0, keepdims=True))
        run = run + jnp.sum(onehot, axis=1, keepdims=True)
    return jnp.concatenate(rows, axis=0)


def _row_windows(n, emit):
    n_big = n // SEG_BIG

    def big(j, _):
        emit(j * SEG_BIG, SEG_BIG)
        return 0

    lax.fori_loop(0, n_big, big, 0)
    off = n_big * SEG_BIG
    for bit in range(SEG_BIG.bit_length() - 2, SEG_ALIGN.bit_length() - 2, -1):
        size = 1 << bit
        take = (n >> bit) & 1

        @pl.when(take == 1)
        def _(off=off, size=size):
            emit(off, size)

        off = off + take * size


def _segment_copies(cnt_ref, seg_ref, row_ref, blk, n_exp, vmem, hbm, sem, *, to_hbm, wait):
    def per_expert(e, _):
        src0 = seg_ref[blk * n_exp + e]
        dst0 = row_ref[blk * n_exp + e]

        def window(off, size):
            v = vmem.at[pl.ds(pl.multiple_of(src0 + off, SEG_ALIGN), size), :]
            h = hbm.at[pl.ds(pl.multiple_of(dst0 + off, SEG_ALIGN), size), :]
            cp = pltpu.make_async_copy(v, h, sem) if to_hbm else pltpu.make_async_copy(h, v, sem)
            if wait:
                cp.wait()
            else:
                cp.start()

        _row_windows(cnt_ref[blk * n_exp + e], window)
        return 0

    lax.fori_loop(0, n_exp, per_expert, 0)


def _permutation(pos, n_rows):
    p = lax.broadcasted_iota(jnp.int32, (n_rows, pos.shape[1]), 0)
    out = jnp.where(pos[0:1, :] == p, 1.0, 0.0)
    for k in range(1, pos.shape[0]):
        out = out + jnp.where(pos[k:k + 1, :] == p, 1.0, 0.0)
    return out


def _dispatch_body(cnt_ref, seg_ref, row_ref, fill_ref, ids_ref, gt_ref, segc_ref, tok_ref, xs_out, pos_ref,
                   loc, zeros, sems, *, n_exp):
    blk = pl.program_id(0)
    last = pl.num_programs(0) - 1
    slot = blk % 2
    top_k, tb = ids_ref.shape
    words = tok_ref.shape[1]
    n_rows = loc.shape[1]

    def copies(b, s, wait):
        _segment_copies(cnt_ref, seg_ref, row_ref, b, n_exp, loc.at[s], xs_out, sems.at[s], to_hbm=True, wait=wait)

    @pl.when(blk == 0)
    def _():
        zeros[...] = jnp.zeros_like(zeros)

        def fill(wait):
            def gap(e, _):
                row = fill_ref[e]

                def window(off, size):
                    cp = pltpu.make_async_copy(zeros.at[pl.ds(0, size), :],
                                               xs_out.at[pl.ds(pl.multiple_of(row + off, SEG_ALIGN), size), :],
                                               sems.at[0])
                    if wait:
                        cp.wait()
                    else:
                        cp.start()

                _row_windows(fill_ref[n_exp + 1 + e], window)
                return 0

            lax.fori_loop(0, n_exp + 1, gap, 0)

        fill(False)
        fill(True)

    @pl.when(blk >= 2)
    def _():
        copies(blk - 2, slot, True)

    pos = _local_positions(ids_ref, segc_ref[0], n_exp).astype(jnp.int32)
    pos_ref[...] = pos
    perm = _permutation(pos, n_rows).astype(BF16)
    lo, hi = _unpack_bf16_pairs(tok_ref[...])
    lo_bits = lax.bitcast_convert_type(_dot(perm, lo), jnp.uint32)
    hi_bits = lax.bitcast_convert_type(_dot(perm, hi), jnp.uint32)
    buf = loc.at[slot]
    buf[:, 0:words] = (lo_bits >> 16) | hi_bits
    p = lax.broadcasted_iota(jnp.int32, (n_rows, tb), 0)
    gates = gt_ref[...]
    spread = jnp.where(pos[0:1, :] == p, gates[0:1, :], 0.0)
    for k in range(1, top_k):
        spread = spread + jnp.where(pos[k:k + 1, :] == p, gates[k:k + 1, :], 0.0)
    gate_row = jnp.sum(spread, axis=1, keepdims=True)
    buf[:, words:] = lax.bitcast_convert_type(jnp.broadcast_to(gate_row, (n_rows, LANES)), jnp.uint32)
    copies(blk, slot, False)

    @pl.when(blk == last)
    def _():
        @pl.when(blk >= 1)
        def _():
            copies(blk - 1, 1 - slot, True)

        copies(blk, slot, True)


def _expert_body(be_ref, nu_ref, x_ref, wgu_ref, bgu_ref, wd_ref, bd_ref, o_ref, wgu_b, wd_b, *, d_ff):
    i = pl.program_id(0)
    used = i < nu_ref[0]

    @pl.when(jnp.logical_and(used, jnp.logical_or(i == 0, be_ref[i] != be_ref[jnp.maximum(i - 1, 0)])))
    def _():
        wgu_b[...] = wgu_ref[0].astype(BF16)
        wd_b[...] = wd_ref[0].astype(BF16)

    @pl.when(used)
    def _():
        x = x_ref[...]
        half = x.shape[1] - LANES
        lo, hi = _unpack_bf16_pairs(x[:, :half])
        gate = lax.bitcast_convert_type(x[:, half:half + 1], F32)
        h = _dot(lo, wgu_b[:half, :]) + _dot(hi, wgu_b[half:, :]) + bgu_ref[0]
        g = jnp.minimum(h[:, :d_ff], SWIGLU_LIMIT)
        u = jnp.clip(h[:, d_ff:], -SWIGLU_LIMIT, SWIGLU_LIMIT)
        act = (u + 1.0) * (g * jax.nn.sigmoid(SWIGLU_ALPHA * g))
        out = (_dot(act.astype(BF16), wd_b[...]) + bd_ref[0]) * gate
        o_ref[...] = _pack_bf16_pairs(out)

    @pl.when(jnp.logical_not(used))
    def _():
        o_ref[...] = jnp.zeros_like(o_ref)


def _combine_body(cnt_ref, seg_ref, row_ref, pos_ref, h_ref, nf_ref, os_hbm, y_ref, loc, sems, *, n_exp, blk_off):
    i = pl.program_id(0)
    slot = i % 2
    n_rows = loc.shape[1]
    n_pairs = pos_ref.shape[0] * pos_ref.shape[1]

    def fetch(step, s, wait):
        _segment_copies(cnt_ref, seg_ref, row_ref, step + blk_off, n_exp, loc.at[s], os_hbm, sems.at[s],
                        to_hbm=False, wait=wait)

    def prepare(step, s):
        loc[s, n_pairs:, :] = jnp.zeros((n_rows - n_pairs, loc.shape[2]), loc.dtype)
        fetch(step, s, False)

    @pl.when(i == 0)
    def _():
        prepare(0, 0)

    @pl.when(i + 1 < pl.num_programs(0))
    def _():
        prepare(i + 1, 1 - slot)

    perm = _permutation(pos_ref[...], n_rows).astype(BF16)
    fetch(i, slot, True)
    lo, hi = _unpack_bf16_pairs(loc[slot])
    f = jnp.concatenate([lax.dot_general(perm, lo, _TN, preferred_element_type=F32),
                         lax.dot_general(perm, hi, _TN, preferred_element_type=F32)], axis=1)
    y_ref[...] = _rms(h_ref[...] + f, nf_ref[...])


def _largest_pow2_divisor(n, cap):
    t = cap
    while n % t:
        t //= 2
    return t


def _rotation_tables(pos, half):
    inv_freq = 1.0 / (ROPE_BASE ** jnp.linspace(0.0, 1.0, half, dtype=F32))
    ang = pos.astype(F32)[:, None] * inv_freq[None, :]
    cos, sin = jnp.cos(ang), jnp.sin(ang)
    return jnp.concatenate([cos, cos], axis=1), jnp.concatenate([-sin, sin], axis=1)


def _decay_tables(log_g, c):
    idx = jnp.arange(c, dtype=F32)
    lg = log_g[:, None]
    rel = idx[:, None] - idx[None, :]
    dec = jnp.where(rel >= 0, jnp.exp(lg[:, :, None] * jnp.maximum(rel, 0.0)), 0.0)
    rowdec = jnp.exp(lg * (idx + 1.0))[:, :, None]
    coldec = jnp.exp(lg * (c - 1.0 - idx))[:, :, None]
    gc = jnp.exp(lg * c)[:, :, None]
    return dec, rowdec, coldec, gc


def kernel(x_prompt, x_sample, cache_sb_k, cache_sb_v, state_ret, meta_tokens, norm_mix, w_in, w_branch_sb, w_branch_ret, w_out, norm_ffn, w_router, b_router, w_gate_up, b_gate_up, w_down, b_down, norm_final):
    B, SEQ, D = x_prompt.shape
    Bs, Ts, _ = x_sample.shape
    depth, _, past, h_sb, dh = cache_sb_k.shape
    _, _, h_ret, dk, dv = state_ret.shape
    n_meta = meta_tokens.shape[0]
    n_exp = w_router.shape[-1]
    d_ff = w_down.shape[-2]
    sbw, rqk, rv = h_sb * dh, h_ret * dk, h_ret * dv
    assert depth == 1, "one layer: the meta-token rows feed nothing after the mixers"
    assert dk == LANES and 2 * dh == LANES and h_sb * Ts == LANES and n_meta % 16 == 0

    TQ = _largest_pow2_divisor(SEQ, 512)
    assert TQ >= 2 * LANES and n_meta <= LANES
    t_p, t_s = B * SEQ, Bs * Ts
    t_moe = t_p + t_s
    TM = _largest_pow2_divisor(t_moe, 512)
    assert TM >= 16 and SEQ % TM == 0 and t_p % TM == 0 and t_moe % n_meta == 0 and t_p % Ts == 0
    n_p, n_moe = t_p // TM, t_moe // TM
    t_x = t_s + n_meta
    meta_blk = t_s // n_meta

    x_p = x_prompt.reshape(t_p, D)
    x_x = jnp.concatenate([x_sample.reshape(t_s, D), meta_tokens.astype(F32)], axis=0)

    sizes = (sbw,) * 3 + (rqk,) * 2 + (rv,) * 2 + (D,) * 2
    offs = tuple(sum(sizes[:i]) for i in range(len(sizes) + 1))
    n_sq = SEQ // TM
    row = lambda w: pl.BlockSpec((TM, w), lambda i: (i, 0))
    out_dt = (BF16, F32, F32, F32, F32, BF16, F32, F32, F32)
    w_in_b, g_mix = w_in[0].astype(BF16), norm_mix[0][None, :]

    def inproj(x, pos, tm, tab_period):
        rows = x.shape[0]
        blk = lambda w: pl.BlockSpec((tm, w), lambda i: (i, 0))
        tab = pl.BlockSpec((tm, dk), lambda i: (i % tab_period, 0))
        return pl.pallas_call(
            functools.partial(_inproj_body, offs=offs, q_scale=dh ** -0.5, k_scale=dk ** -0.5, n_ret=h_ret, dk=dk),
            grid=(rows // tm,),
            in_specs=[blk(D), pl.BlockSpec((1, D), lambda i: (0, 0)),
                      pl.BlockSpec((D, offs[-1]), lambda i: (0, 0), pipeline_mode=pl.Buffered(1)), tab, tab],
            out_specs=[blk(w) for w in sizes],
            out_shape=[jax.ShapeDtypeStruct((rows, w), dt) for w, dt in zip(sizes, out_dt)],
            compiler_params=_params("parallel"), name="inproj",
        )(x, g_mix, w_in_b, *_rotation_tables(pos, dk // 2))

    qa, ka, va, qr, kr, vr, gr, ga, gb = inproj(x_p, jnp.arange(SEQ), TM, n_sq)
    x_pos = jnp.concatenate([jnp.tile(past + jnp.arange(Ts), Bs), jnp.arange(n_meta) - n_meta])
    qa_x, ka_x, va_x, qr_x, kr_x, vr_x, gr_x, ga_x, gb_x = inproj(x_x, x_pos, t_x, 1)

    n_q = SEQ // TQ
    n_pair = sbw // LANES
    ya_p = pl.pallas_call(
        functools.partial(_sb_prompt_body, tq=TQ, n_meta=n_meta, dh=dh),
        grid=(B, n_pair, n_q),
        in_specs=[pl.BlockSpec((TQ, LANES), lambda b, p, q: (b * n_q + q, p)),
                  pl.BlockSpec((SEQ, LANES), lambda b, p, q: (b, p)),
                  pl.BlockSpec((SEQ, LANES), lambda b, p, q: (b, p)),
                  pl.BlockSpec((n_meta, LANES), lambda b, p, q: (meta_blk, p)),
                  pl.BlockSpec((n_meta, LANES), lambda b, p, q: (meta_blk, p))],
        out_specs=pl.BlockSpec((TQ, LANES), lambda b, p, q: (b * n_q + q, p)),
        out_shape=jax.ShapeDtypeStruct((t_p, sbw), BF16),
        scratch_shapes=[pltpu.VMEM((LANES + SEQ, LANES), BF16), pltpu.VMEM((LANES + SEQ, LANES), BF16),
                        pltpu.VMEM((2 * TQ, 1), F32), pltpu.VMEM((2 * TQ, LANES), F32)],
        compiler_params=_params("parallel", "parallel", "arbitrary"), name="sb_prompt",
    )(qa, ka, va, ka_x, va_x)

    KC = _largest_pow2_divisor(past, 1024)
    SUB = min(KC, 256)
    n_kc = past // KC
    smp = lambda w: pl.BlockSpec((Ts, w), lambda b, j: (b, 0))
    met = lambda w: pl.BlockSpec((n_meta, w), lambda b, j: (meta_blk, 0))
    cache = pl.BlockSpec((1, KC, sbw), lambda b, j: (b, n_kc - 1 - j, 0))
    ya_s = pl.pallas_call(
        functools.partial(_sb_sample_body, ts=Ts, n_meta=n_meta, dh=dh, sub=SUB),
        grid=(Bs, n_kc),
        in_specs=[smp(sbw), smp(sbw), smp(sbw), met(sbw), met(sbw), cache, cache],
        out_specs=pl.BlockSpec((Ts, sbw), lambda b, j: (b, 0)),
        out_shape=jax.ShapeDtypeStruct((t_s, sbw), BF16),
        scratch_shapes=[pltpu.VMEM((LANES, sbw), BF16), pltpu.VMEM((1, LANES), F32), pltpu.VMEM((LANES, sbw), F32)],
        compiler_params=_params("parallel", "arbitrary"), name="sb_sample",
    )(qa_x, ka_x, va_x, ka_x, va_x, cache_sb_k[0].reshape(Bs, past, sbw), cache_sb_v[0].reshape(Bs, past, sbw))

    log_g = jnp.log(1.0 - 2.0 ** (-5.0 - jnp.arange(h_ret, dtype=F32)))
    RC = _largest_pow2_divisor(SEQ, 256)
    dec, rd, cd, gc = _decay_tables(log_g, RC)
    _, _, mcd, _ = _decay_tables(log_g, n_meta)
    per_b = lambda w: pl.BlockSpec((SEQ, w), lambda b, h: (b, h))
    met_h = lambda w: pl.BlockSpec((n_meta, w), lambda b, h: (meta_blk, h))
    tab = lambda a: pl.BlockSpec((1,) + a.shape[1:], lambda b, h: (h, 0, 0))
    state_spec = pl.BlockSpec((1, 1, dk, dv), lambda b, h: (b, h, 0, 0))
    yr_p, s_p = pl.pallas_call(
        functools.partial(_ret_prompt_body, chunk=RC),
        grid=(B, h_ret),
        in_specs=[per_b(dk), per_b(dk), per_b(dv), per_b(dv), met_h(dk), met_h(dv),
                  tab(dec), tab(rd), tab(cd), tab(gc), tab(mcd)],
        out_specs=[per_b(dv), state_spec],
        out_shape=[jax.ShapeDtypeStruct((t_p, rv), BF16), jax.ShapeDtypeStruct((B, h_ret, dk, dv), F32)],
        compiler_params=_params("parallel", "parallel"), name="ret_prompt",
    )(qr, kr, vr, gr, kr_x, vr_x, dec, rd, cd, gc, mcd)

    dec_s, rd_s, cd_s, gc_s = _decay_tables(log_g, Ts)
    smp_h = lambda w: pl.BlockSpec((Ts, w), lambda b, h: (b, h))
    yr_s, s_s = pl.pallas_call(
        _ret_sample_body,
        grid=(Bs, h_ret),
        in_specs=[smp_h(dk), smp_h(dk), smp_h(dv), smp_h(dv), state_spec,
                  tab(dec_s), tab(rd_s), tab(cd_s), tab(gc_s)],
        out_specs=[smp_h(dv), state_spec],
        out_shape=[jax.ShapeDtypeStruct((t_s, rv), BF16), jax.ShapeDtypeStruct((Bs, h_ret, dk, dv), F32)],
        compiler_params=_params("parallel", "parallel"), name="ret_sample",
    )(qr_x, kr_x, vr_x, gr_x, state_ret[0], dec_s, rd_s, cd_s, gc_s)

    const = lambda a: pl.BlockSpec(a.shape, lambda i: (0,) * a.ndim)
    prm = lambda w: pl.BlockSpec((TM, w), lambda i: (jnp.minimum(i, n_p - 1), 0))
    smp_m = lambda w: pl.BlockSpec((TM, w), lambda i: (jnp.maximum(i - n_p, 0), 0))
    lane_major = pl.BlockSpec((TOP_K, TM), lambda i: (0, i))
    wa, wb, wo = w_branch_sb[0].astype(BF16), w_branch_ret[0].astype(BF16), w_out[0].astype(BF16)
    nf, wr_t, br = norm_ffn[0][None, :], w_router[0].T.astype(BF16), b_router[0][:, None]
    h1, tok, ids, gates, cnt = pl.pallas_call(
        functools.partial(_merge_body, n_prompt=n_p, top_k=TOP_K),
        grid=(n_moe,),
        in_specs=[prm(sbw), prm(rv), prm(D), prm(D), prm(D), smp_m(sbw), smp_m(rv), smp_m(D), smp_m(D), smp_m(D),
                  const(wa), const(wb), const(wo), const(nf), const(wr_t), const(br)],
        out_specs=[row(D), row(D // 2), lane_major, lane_major, pl.BlockSpec((1, n_exp, 1), lambda i: (i, 0, 0))],
        out_shape=[jax.ShapeDtypeStruct((t_moe, D), F32), jax.ShapeDtypeStruct((t_moe, D // 2), jnp.uint32),
                   jax.ShapeDtypeStruct((TOP_K, t_moe), jnp.int32), jax.ShapeDtypeStruct((TOP_K, t_moe), F32),
                   jax.ShapeDtypeStruct((n_moe, n_exp, 1), F32)],
        compiler_params=_params("parallel"), name="merge_router",
    )(ya_p, yr_p, ga, gb, x_p, ya_s, yr_s, ga_x, gb_x, x_x, wa, wb, wo, nf, wr_t, br)

    BM = TM
    cnt = cnt[:, :, 0].astype(jnp.int32)
    cnt = (cnt + SEG_ALIGN - 1) // SEG_ALIGN * SEG_ALIGN
    padded = (jnp.sum(cnt, axis=0) + BM - 1) // BM * BM
    pad_end = jnp.cumsum(padded)
    seg_row = (pad_end - padded)[None, :] + jnp.cumsum(cnt, axis=0) - cnt
    seg_loc = jnp.cumsum(cnt, axis=1) - cnt
    n_blk = -(-(t_moe * TOP_K + (SEG_ALIGN - 1) * n_moe * n_exp) // BM) + n_exp
    block_e = jnp.minimum(jnp.sum(pad_end[None, :] <= (jnp.arange(n_blk) * BM)[:, None], axis=1), n_exp - 1).astype(jnp.int32)
    n_used = (pad_end[-1:] // BM).astype(jnp.int32)
    seg_tabs = (cnt.reshape(-1), seg_loc.reshape(-1).astype(jnp.int32), seg_row.reshape(-1).astype(jnp.int32))
    grp_end = pad_end - padded + jnp.sum(cnt, axis=0)
    fill_rows = jnp.concatenate([grp_end, pad_end[-1:], pad_end - grp_end,
                                 (n_blk + 1) * BM - pad_end[-1:]]).astype(jnp.int32)

    XW = D // 2 + LANES
    n_loc = -(-(TOP_K * TM + SEG_ALIGN * n_exp) // LANES) * LANES
    xs, pos = pl.pallas_call(
        functools.partial(_dispatch_body, n_exp=n_exp),
        grid_spec=pltpu.PrefetchScalarGridSpec(
            num_scalar_prefetch=4, grid=(n_moe,),
            in_specs=[pl.BlockSpec((TOP_K, TM), lambda i, *_: (0, i)), pl.BlockSpec((TOP_K, TM), lambda i, *_: (0, i)),
                      pl.BlockSpec((1, n_exp, 1), lambda i, *_: (i, 0, 0)),
                      pl.BlockSpec((TM, D // 2), lambda i, *_: (i, 0))],
            out_specs=[pl.BlockSpec(memory_space=pl.ANY), pl.BlockSpec((TOP_K, TM), lambda i, *_: (0, i))],
            scratch_shapes=[pltpu.VMEM((2, n_loc, XW), jnp.uint32), pltpu.VMEM((SEG_BIG, XW), jnp.uint32),
                            pltpu.SemaphoreType.DMA((2,))]),
        out_shape=[jax.ShapeDtypeStruct(((n_blk + 1) * BM, XW), jnp.uint32),
                   jax.ShapeDtypeStruct((TOP_K, t_moe), jnp.int32)],
        compiler_params=_params("arbitrary"), name="moe_dispatch",
    )(*seg_tabs, fill_rows, ids, gates, seg_loc.astype(F32)[:, :, None], tok)

    blk = lambda i, be, nu: jnp.minimum(i, nu[0] - 1)
    exp_w = lambda a: pl.BlockSpec((1,) + a.shape[1:], lambda i, be, nu: (be[blk(i, be, nu)],) + (0,) * (a.ndim - 1))
    wgu, bgu = w_gate_up[0], b_gate_up[0][:, None, :]
    wd, bd = w_down[0], b_down[0][:, None, :]
    os_ = pl.pallas_call(
        functools.partial(_expert_body, d_ff=d_ff),
        grid_spec=pltpu.PrefetchScalarGridSpec(
            num_scalar_prefetch=2, grid=(n_blk,),
            in_specs=[pl.BlockSpec((BM, XW), lambda i, be, nu: (blk(i, be, nu), 0)),
                      exp_w(wgu), exp_w(bgu), exp_w(wd), exp_w(bd)],
            out_specs=pl.BlockSpec((BM, D // 2), lambda i, be, nu: (i, 0)),
            scratch_shapes=[pltpu.VMEM(wgu.shape[1:], BF16), pltpu.VMEM(wd.shape[1:], BF16)]),
        out_shape=jax.ShapeDtypeStruct((n_blk * BM, D // 2), jnp.uint32),
        compiler_params=_params("arbitrary"), name="moe_experts",
    )(block_e, n_used, xs, wgu, bgu, wd, bd)

    def combine(off, n):
        return pl.pallas_call(
            functools.partial(_combine_body, n_exp=n_exp, blk_off=off),
            grid_spec=pltpu.PrefetchScalarGridSpec(
                num_scalar_prefetch=3, grid=(n,),
                in_specs=[pl.BlockSpec((TOP_K, TM), lambda i, *_: (0, i + off)),
                          pl.BlockSpec((TM, D), lambda i, *_: (i + off, 0)),
                          pl.BlockSpec((1, D), lambda i, *_: (0, 0)), pl.BlockSpec(memory_space=pl.ANY)],
                out_specs=pl.BlockSpec((TM, D), lambda i, *_: (i, 0)),
                scratch_shapes=[pltpu.VMEM((2, n_loc, D // 2), jnp.uint32), pltpu.SemaphoreType.DMA((2,))]),
            out_shape=jax.ShapeDtypeStruct((n * TM, D), F32),
            compiler_params=_params("arbitrary"), name="moe_combine",
        )(*seg_tabs, pos, h1, norm_final[None, :], os_)

    y_prompt = combine(0, n_p).reshape(B, SEQ, D)
    y_sample = combine(n_p, n_moe - n_p).reshape(Bs, Ts, D)

    def with_meta(a, a_x):
        meta_rows = jnp.broadcast_to(a_x[t_s:].reshape(1, n_meta, h_sb, dh), (B, n_meta, h_sb, dh))
        return jnp.concatenate([meta_rows, a.reshape(B, SEQ, h_sb, dh)], axis=1)[None]

    return (y_prompt, y_sample, with_meta(ka, ka_x), with_meta(va, va_x), s_p[None],
            ka_x[:t_s].reshape(1, Bs, Ts, h_sb, dh), va_x[:t_s].reshape(1, Bs, Ts, h_sb, dh), s_s[None])
```

```python
import functools

import jax
import jax.numpy as jnp
from jax import lax
from jax.experimental import pallas as pl
from jax.experimental.pallas import tpu as pltpu

F32 = jnp.float32
BF16 = jnp.bfloat16

EPS = 1e-6
ROPE_BASE = 10000.0
TOP_K = 4
SWIGLU_LIMIT = 7.0
SWIGLU_ALPHA = 1.702
LOG2E = 1.4426950408889634

LANES = 128
SEG_ALIGN = 8
SEG_BIG = 256
VMEM_LIMIT = 56 << 20

_NT = (((1,), (1,)), ((), ()))
_TN = (((0,), (0,)), ((), ()))


def _params(*sem, **kw):
    return pltpu.CompilerParams(dimension_semantics=sem, vmem_limit_bytes=VMEM_LIMIT, **kw)


def _dot(a, b):
    return jnp.dot(a, b, preferred_element_type=F32)


def _rms(x, g):
    return x * lax.rsqrt(jnp.mean(x * x, axis=-1, keepdims=True) + EPS) * g


def _inproj_body(x_ref, g_ref, w_ref, cs_ref, sn_ref, qa_ref, ka_ref, va_ref, qr_ref, kr_ref,
                 vr_ref, gr_ref, ga_ref, gb_ref, *, offs, q_scale, k_scale, n_ret, dk):
    xn = _rms(x_ref[...], g_ref[...]).astype(BF16)

    def proj(i):
        return _dot(xn, w_ref[:, offs[i]:offs[i + 1]])

    qa_ref[...] = (proj(0) * q_scale).astype(BF16)
    ka_ref[...] = proj(1)
    va_ref[...] = proj(2)
    cs = cs_ref[...]
    sn = sn_ref[...]
    for ref, i, scale in ((qr_ref, 3, None), (kr_ref, 4, k_scale)):
        t = proj(i)
        for h in range(n_ret):
            th = t[:, h * dk:(h + 1) * dk]
            r = th * cs + pltpu.roll(th, dk // 2, axis=1) * sn
            ref[:, h * dk:(h + 1) * dk] = r if scale is None else r * scale
    vr_ref[...] = proj(5).astype(BF16)
    gr_ref[...] = proj(6)
    ga_ref[...] = proj(7)
    gb_ref[...] = proj(8)


def _sb_scores(z):
    log_beta = jnp.minimum(z, 0.0) - jnp.log(1.0 + jnp.exp(-jnp.abs(z)))
    return log_beta, log_beta - z


def _split_bf16(x):
    hi = x.astype(BF16)
    return hi, (x - hi.astype(F32)).astype(BF16)


def _suffix_ones(n):
    r = lax.broadcasted_iota(jnp.int32, (2 * n, n), 0) & (n - 1)
    c = lax.broadcasted_iota(jnp.int32, (2 * n, n), 1)
    return jnp.where(r >= c, 1.0, 0.0).astype(BF16)


def _sb_logits(qs, kblk):
    return lax.dot_general(qs, kblk, _NT, preferred_element_type=F32) * LOG2E


def _sb_suffix(z2, suffix2, mask):
    neg_abs = lax.bitcast_convert_type(lax.bitcast_convert_type(z2, jnp.uint32) | jnp.uint32(0x80000000), F32)
    drop = jnp.maximum(z2, 0.0) + jnp.log2(1.0 + jnp.exp2(neg_abs))
    if mask is not None:
        drop = jnp.where(mask, drop, 0.0)
    hi, lo = _split_bf16(drop)
    return _dot(jnp.concatenate([hi, lo], axis=1), suffix2)


def _sb_weights(z2, incl, carry, mask):
    w = jnp.exp2(z2 - (incl if carry is None else incl + carry))
    if mask is not None:
        w = jnp.where(mask, w, 0.0)
    return w.astype(BF16)


def _sb_block(qs, kblk, vblk, suffix2, mask, carry, acc):
    z2 = _sb_logits(qs, kblk)
    incl = _sb_suffix(z2, suffix2, mask)
    return carry + incl[:, 0:1], acc + _dot(_sb_weights(z2, incl, carry, mask), vblk)


def _sb_prompt_body(q_ref, k_ref, v_ref, mk_ref, mv_ref, o_ref, ko_ref, vo_ref, kb, vb, carry_ref, acc_ref, *,
                    tq, n_meta, dh):
    qi = pl.program_id(2)
    tk = tq // 2
    mpad = LANES - n_meta
    n_heads = LANES // dh

    @pl.when(qi == 0)
    def _():
        ko_ref[0, 0:n_meta, :] = mk_ref[...]
        vo_ref[0, 0:n_meta, :] = mv_ref[...]
        ko_ref[0, n_meta:, :] = k_ref[...]
        vo_ref[0, n_meta:, :] = v_ref[...]
        kb[0:mpad, :] = jnp.zeros((mpad, LANES), BF16)
        vb[0:mpad, :] = jnp.zeros((mpad, LANES), BF16)
        kb[mpad:LANES, :] = mk_ref[...].astype(BF16)
        vb[mpad:LANES, :] = mv_ref[...].astype(BF16)
        kb[LANES:, :] = k_ref[...].astype(BF16)
        vb[LANES:, :] = v_ref[...].astype(BF16)

    q = q_ref[...]
    lane = lax.broadcasted_iota(jnp.int32, (tk, LANES), 1)
    parts = []
    for half in range(2):
        qh = q[half * tk:(half + 1) * tk]
        for head in range(n_heads):
            parts.append(jnp.where(lane // dh == head, qh, jnp.zeros_like(qh)))
    qs = jnp.concatenate(parts, axis=0)
    m_half = n_heads * tk
    suffix2 = _suffix_ones(tk)
    base = pl.multiple_of(LANES + qi * tq, LANES)

    r = lax.broadcasted_iota(jnp.int32, (m_half, tk), 0) & (tk - 1)
    c = lax.broadcasted_iota(jnp.int32, (m_half, tk), 1)
    mask_b = c < r
    r = lax.broadcasted_iota(jnp.int32, (2 * m_half, tk), 0)
    c = lax.broadcasted_iota(jnp.int32, (2 * m_half, tk), 1)
    mask_a = (r >= m_half) | (c < (r & (tk - 1)))
    z_b = _sb_logits(qs[m_half:], kb[pl.ds(base + tk, tk), :])
    z_a = _sb_logits(qs, kb[pl.ds(base, tk), :])
    incl_b = _sb_suffix(z_b, suffix2, mask_b)
    incl_a = _sb_suffix(z_a, suffix2, mask_a)
    acc_b = _dot(_sb_weights(z_b, incl_b, None, mask_b), vb[pl.ds(base + tk, tk), :])
    carry = jnp.concatenate([jnp.zeros((m_half, 1), F32), incl_b[:, 0:1]], axis=0)
    acc = jnp.concatenate([jnp.zeros((m_half, LANES), F32), acc_b], axis=0)
    acc_ref[...] = acc + _dot(_sb_weights(z_a, incl_a, carry, mask_a), vb[pl.ds(base, tk), :])
    carry_ref[...] = carry + incl_a[:, 0:1]

    def left(it, _):
        offs = [pl.multiple_of(base - (2 * it + step) * tk, LANES) for step in (1, 2)]
        z2 = [_sb_logits(qs, kb[pl.ds(o, tk), :]) for o in offs]
        incl = [_sb_suffix(z, suffix2, None) for z in z2]
        carry, acc = carry_ref[...], acc_ref[...]
        for z, inc, o in zip(z2, incl, offs):
            acc = acc + _dot(_sb_weights(z, inc, carry, None), vb[pl.ds(o, tk), :])
            carry = carry + inc[:, 0:1]
        carry_ref[...] = carry
        acc_ref[...] = acc
        return 0

    lax.fori_loop(0, qi, left, 0)
    c = lax.broadcasted_iota(jnp.int32, (2 * m_half, LANES), 1)
    _, acc = _sb_block(qs, kb[0:LANES, :], vb[0:LANES, :], _suffix_ones(LANES), c >= mpad,
                       carry_ref[...], acc_ref[...])
    for half in range(2):
        for head in range(n_heads):
            rows = (half * n_heads + head) * tk
            o_ref[half * tk:(half + 1) * tk, head * dh:(head + 1) * dh] = (
                acc[rows:rows + tk, head * dh:(head + 1) * dh].astype(o_ref.dtype))


def _sb_sample_body(q_ref, nk_ref, nv_ref, mk_ref, mv_ref, ck_ref, cv_ref, o_ref,
                    qbd, carry_ref, acc_ref, *, ts, n_meta, dh, sub):
    j = pl.program_id(1)
    width = q_ref.shape[1]
    n_heads = width // dh

    def head_mask(shape):
        r = lax.broadcasted_iota(jnp.int32, shape, 0)
        c = lax.broadcasted_iota(jnp.int32, shape, 1)
        return r // ts == c // dh

    def sweep(k, v, mask, carry, acc):
        n = k.shape[0]
        z = lax.dot_general(k.astype(BF16), qbd[...], _NT, preferred_element_type=F32)
        log_beta, log_keep = _sb_scores(z)
        if mask is not None:
            log_keep = jnp.where(mask, log_keep, 0.0)
        r = lax.broadcasted_iota(jnp.int32, (n, n), 0)
        c = lax.broadcasted_iota(jnp.int32, (n, n), 1)
        suffix = jnp.where(c >= r, 1.0, 0.0).astype(BF16)
        hi, lo = _split_bf16(log_keep)
        incl = _dot(suffix, hi) + _dot(suffix, lo)
        w = jnp.exp(log_beta + (carry + incl - log_keep))
        if mask is not None:
            w = jnp.where(mask, w, 0.0)
        acc = acc + lax.dot_general(w.astype(BF16), v.astype(BF16), _TN, preferred_element_type=F32)
        return carry + incl[0:1, :], acc

    @pl.when(j == 0)
    def _():
        qt = jnp.concatenate([q_ref[...]] * n_heads, axis=0)
        qbd[...] = jnp.where(head_mask((n_heads * ts, width)), qt, jnp.zeros_like(qt))
        key = lax.broadcasted_iota(jnp.int32, (ts, LANES), 0)
        qry = lax.broadcasted_iota(jnp.int32, (ts, LANES), 1) % ts
        carry, acc = sweep(nk_ref[...], nv_ref[...], key < qry,
                           jnp.zeros((1, LANES), F32), jnp.zeros((LANES, width), F32))
        carry_ref[...] = carry
        acc_ref[...] = acc

    carry = carry_ref[...]
    acc = acc_ref[...]
    chunk = ck_ref.shape[1]
    for s in range(chunk // sub - 1, -1, -1):
        carry, acc = sweep(ck_ref[0, s * sub:(s + 1) * sub, :], cv_ref[0, s * sub:(s + 1) * sub, :],
                           None, carry, acc)
    carry_ref[...] = carry
    acc_ref[...] = acc

    @pl.when(j == pl.num_programs(1) - 1)
    def _():
        _, acc = sweep(mk_ref[...], mv_ref[...], None, carry_ref[...], acc_ref[...])
        own = jnp.where(head_mask((n_heads * ts, width)), acc, 0.0)
        o_ref[...] = jnp.sum(own.reshape(n_heads, ts, width), axis=0).astype(o_ref.dtype)


def _ret_chunk(q, k, v, state, dec, rowdec, coldec, gc):
    qb = q.astype(BF16)
    inner = lax.dot_general(qb, k.astype(BF16), _NT, preferred_element_type=F32) * dec
    o = _dot(inner.astype(BF16), v) + _dot(qb, state.astype(BF16)) * rowdec
    kw = (k * coldec).astype(BF16)
    return gc * state + lax.dot_general(kw, v, _TN, preferred_element_type=F32), o


def _ret_gate(o, g):
    on = o * lax.rsqrt(jnp.mean(o * o, axis=-1, keepdims=True) + EPS)
    return (g * jax.nn.sigmoid(g)) * on


def _ret_prompt_body(q_ref, k_ref, v_ref, g_ref, mk_ref, mv_ref, dec_ref, rd_ref, cd_ref, gc_ref,
                     mcd_ref, y_ref, s_ref, *, chunk):
    kw = (mk_ref[...] * mcd_ref[0]).astype(BF16)
    state = lax.dot_general(kw, mv_ref[...], _TN, preferred_element_type=F32)
    dec, rd, cd, gc = dec_ref[0], rd_ref[0], cd_ref[0], gc_ref[0]

    def step(c, state):
        r = pl.ds(pl.multiple_of(c * chunk, chunk), chunk)
        state, o = _ret_chunk(q_ref[r, :], k_ref[r, :], v_ref[r, :], state, dec, rd, cd, gc)
        y_ref[r, :] = _ret_gate(o, g_ref[r, :]).astype(y_ref.dtype)
        return state

    s_ref[0, 0] = lax.fori_loop(0, q_ref.shape[0] // chunk, step, state)


def _ret_sample_body(q_ref, k_ref, v_ref, g_ref, s0_ref, dec_ref, rd_ref, cd_ref, gc_ref,
                     y_ref, s_ref):
    state, o = _ret_chunk(q_ref[...], k_ref[...], v_ref[...], s0_ref[0, 0], dec_ref[0], rd_ref[0],
                          cd_ref[0], gc_ref[0])
    y_ref[...] = _ret_gate(o, g_ref[...]).astype(y_ref.dtype)
    s_ref[0, 0] = state


def _pack_bf16_pairs(x):
    bits = lax.bitcast_convert_type(x.astype(BF16).astype(F32), jnp.uint32)
    w = x.shape[1] // 2
    return (bits[:, :w] >> 16) | bits[:, w:]


def _unpack_bf16_pairs(p):
    lo = lax.bitcast_convert_type(p << 16, F32).astype(BF16)
    hi = lax.bitcast_convert_type(p & jnp.uint32(0xFFFF0000), F32).astype(BF16)
    return lo, hi


def _merge_body(*refs, n_prompt, top_k):
    prompt, sample, rest = refs[0:5], refs[5:10], refs[10:]
    i = pl.program_id(0)

    @pl.when(i < n_prompt)
    def _():
        _merge_block(*prompt, *rest, top_k=top_k)

    @pl.when(i >= n_prompt)
    def _():
        _merge_block(*sample, *rest, top_k=top_k)


def _merge_block(ya_ref, yr_ref, ga_ref, gb_ref, x_ref, wa_ref, wb_ref, wo_ref, nf_ref, wr_ref, br_ref,
                 h_ref, tok_ref, ids_ref, gt_ref, cnt_ref, *, top_k):
    mix = (jax.nn.sigmoid(ga_ref[...]) * _dot(ya_ref[...], wa_ref[...])
           + jax.nn.sigmoid(gb_ref[...]) * _dot(yr_ref[...], wb_ref[...]))
    h = x_ref[...] + _dot(mix.astype(BF16), wo_ref[...])
    h_ref[...] = h
    tok = _rms(h, nf_ref[...])
    tok_ref[...] = _pack_bf16_pairs(tok)
    logits = lax.dot_general(wr_ref[...], tok.astype(BF16), _NT, preferred_element_type=F32) + br_ref[...]
    n_exp, tm = logits.shape
    eidx = lax.broadcasted_iota(jnp.int32, (n_exp, tm), 0).astype(F32)
    ids, tops = [], []
    for _ in range(top_k):
        m = jnp.max(logits, axis=0, keepdims=True)
        sel = jnp.min(jnp.where(logits == m, eidx, float(n_exp)), axis=0, keepdims=True)
        ids.append(sel)
        tops.append(m)
        logits = jnp.where(eidx == sel, -jnp.inf, logits)
    ex = [jnp.exp(t - tops[0]) for t in tops]
    inv = 1.0 / functools.reduce(lambda a, b: a + b, ex)
    ids_ref[...] = jnp.concatenate(ids, axis=0).astype(jnp.int32)
    gt_ref[...] = jnp.concatenate([e * inv for e in ex], axis=0)
    cnt = jnp.zeros((n_exp, 1), F32)
    for sel in ids:
        cnt = cnt + jnp.sum(jnp.where(eidx == sel, 1.0, 0.0), axis=1, keepdims=True)
    cnt_ref[0] = cnt


def _local_positions(ids_ref, seg_col, n_exp):
    top_k, tb = ids_ref.shape
    ids = ids_ref[...].astype(F32)
    eidx = lax.broadcasted_iota(jnp.int32, (n_exp, tb), 0).astype(F32)
    r = lax.broadcasted_iota(jnp.int32, (tb, tb), 0)
    c = lax.broadcasted_iota(jnp.int32, (tb, tb), 1)
    before = jnp.where(r < c, 1.0, 0.0).astype(BF16)
    run = seg_col
    rows = []
    for k in range(top_k):
        onehot = jnp.where(ids[k:k + 1, :] == eidx, 1.0, 0.0)
        pos = _dot(onehot.astype(BF16), before) + run
        rows.append(jnp.sum(onehot * pos, axis=0, keepdims=True))
        run = run + jnp.sum(onehot, axis=1, keepdims=True)
    return jnp.concatenate(rows, axis=0)


def _row_windows(n, emit):
    n_big = n // SEG_BIG

    def big(j, _):
        emit(j * SEG_BIG, SEG_BIG)
        return 0

    lax.fori_loop(0, n_big, big, 0)
    off = n_big * SEG_BIG
    for bit in range(SEG_BIG.bit_length() - 2, SEG_ALIGN.bit_length() - 2, -1):
        size = 1 << bit
        take = (n >> bit) & 1

        @pl.when(take == 1)
        def _(off=off, size=size):
            emit(off, size)

        off = off + take * size


def _segment_copies(cnt_ref, seg_ref, row_ref, blk, n_exp, vmem, hbm, sem, *, to_hbm, wait):
    def per_expert(e, _):
        src0 = seg_ref[blk * n_exp + e]
        dst0 = row_ref[blk * n_exp + e]

        def window(off, size):
            v = vmem.at[pl.ds(pl.multiple_of(src0 + off, SEG_ALIGN), size), :]
            h = hbm.at[pl.ds(pl.multiple_of(dst0 + off, SEG_ALIGN), size), :]
            cp = pltpu.make_async_copy(v, h, sem) if to_hbm else pltpu.make_async_copy(h, v, sem)
            if wait:
                cp.wait()
            else:
                cp.start()

        _row_windows(cnt_ref[blk * n_exp + e], window)
        return 0

    lax.fori_loop(0, n_exp, per_expert, 0)


def _permutation(pos, n_rows):
    p = lax.broadcasted_iota(jnp.int32, (n_rows, pos.shape[1]), 0)
    out = jnp.where(pos[0:1, :] == p, 1.0, 0.0)
    for k in range(1, pos.shape[0]):
        out = out + jnp.where(pos[k:k + 1, :] == p, 1.0, 0.0)
    return out


def _dispatch_body(cnt_ref, seg_ref, row_ref, fill_ref, ids_ref, gt_ref, segc_ref, tok_ref, xs_out, pos_ref,
                   loc, zeros, sems, *, n_exp):
    blk = pl.program_id(0)
    last = pl.num_programs(0) - 1
    slot = blk % 2
    top_k, tb = ids_ref.shape
    words = tok_ref.shape[1]
    n_rows = loc.shape[1]

    def copies(b, s, wait):
        _segment_copies(cnt_ref, seg_ref, row_ref, b, n_exp, loc.at[s], xs_out, sems.at[s], to_hbm=True, wait=wait)

    @pl.when(blk == 0)
    def _():
        zeros[...] = jnp.zeros_like(zeros)

        def fill(wait):
            def gap(e, _):
                row = fill_ref[e]

                def window(off, size):
                    cp = pltpu.make_async_copy(zeros.at[pl.ds(0, size), :],
                                               xs_out.at[pl.ds(pl.multiple_of(row + off, SEG_ALIGN), size), :],
                                               sems.at[0])
                    if wait:
                        cp.wait()
                    else:
                        cp.start()

                _row_windows(fill_ref[n_exp + 1 + e], window)
                return 0

            lax.fori_loop(0, n_exp + 1, gap, 0)

        fill(False)
        fill(True)

    @pl.when(blk >= 2)
    def _():
        copies(blk - 2, slot, True)

    pos = _local_positions(ids_ref, segc_ref[0], n_exp).astype(jnp.int32)
    pos_ref[...] = pos
    perm = _permutation(pos, n_rows).astype(BF16)
    lo, hi = _unpack_bf16_pairs(tok_ref[...])
    lo_bits = lax.bitcast_convert_type(_dot(perm, lo), jnp.uint32)
    hi_bits = lax.bitcast_convert_type(_dot(perm, hi), jnp.uint32)
    buf = loc.at[slot]
    buf[:, 0:words] = (lo_bits >> 16) | hi_bits
    p = lax.broadcasted_iota(jnp.int32, (n_rows, tb), 0)
    gates = gt_ref[...]
    spread = jnp.where(pos[0:1, :] == p, gates[0:1, :], 0.0)
    for k in range(1, top_k):
        spread = spread + jnp.where(pos[k:k + 1, :] == p, gates[k:k + 1, :], 0.0)
    gate_row = jnp.sum(spread, axis=1, keepdims=True)
    buf[:, words:] = lax.bitcast_convert_type(jnp.broadcast_to(gate_row, (n_rows, LANES)), jnp.uint32)
    copies(blk, slot, False)

    @pl.when(blk == last)
    def _():
        @pl.when(blk >= 1)
        def _():
            copies(blk - 1, 1 - slot, True)

        copies(blk, slot, True)


def _expert_body(be_ref, nu_ref, x_ref, wgu_ref, bgu_ref, wd_ref, bd_ref, o_ref, wgu_b, wd_b, *, d_ff):
    i = pl.program_id(0)
    used = i < nu_ref[0]

    @pl.when(jnp.logical_and(used, jnp.logical_or(i == 0, be_ref[i] != be_ref[jnp.maximum(i - 1, 0)])))
    def _():
        wgu_b[...] = wgu_ref[0].astype(BF16)
        wd_b[...] = wd_ref[0].astype(BF16)

    @pl.when(used)
    def _():
        x = x_ref[...]
        half = x.shape[1] - LANES
        lo, hi = _unpack_bf16_pairs(x[:, :half])
        gate = lax.bitcast_convert_type(x[:, half:half + 1], F32)
        h = _dot(lo, wgu_b[:half, :]) + _dot(hi, wgu_b[half:, :]) + bgu_ref[0]
        g = jnp.minimum(h[:, :d_ff], SWIGLU_LIMIT)
        u = jnp.clip(h[:, d_ff:], -SWIGLU_LIMIT, SWIGLU_LIMIT)
        act = (u + 1.0) * (g * jax.nn.sigmoid(SWIGLU_ALPHA * g))
        out = (_dot(act.astype(BF16), wd_b[...]) + bd_ref[0]) * gate
        o_ref[...] = _pack_bf16_pairs(out)

    @pl.when(jnp.logical_not(used))
    def _():
        o_ref[...] = jnp.zeros_like(o_ref)


def _combine_body(cnt_ref, seg_ref, row_ref, pos_ref, h_ref, nf_ref, os_hbm, y_ref, loc, sems, *, n_exp, blk_off):
    i = pl.program_id(0)
    slot = i % 2
    n_rows = loc.shape[1]
    n_pairs = pos_ref.shape[0] * pos_ref.shape[1]

    def fetch(step, s, wait):
        _segment_copies(cnt_ref, seg_ref, row_ref, step + blk_off, n_exp, loc.at[s], os_hbm, sems.at[s],
                        to_hbm=False, wait=wait)

    def prepare(step, s):
        loc[s, n_pairs:, :] = jnp.zeros((n_rows - n_pairs, loc.shape[2]), loc.dtype)
        fetch(step, s, False)

    @pl.when(i == 0)
    def _():
        prepare(0, 0)

    @pl.when(i + 1 < pl.num_programs(0))
    def _():
        prepare(i + 1, 1 - slot)

    perm = _permutation(pos_ref[...], n_rows).astype(BF16)
    fetch(i, slot, True)
    lo, hi = _unpack_bf16_pairs(loc[slot])
    f = jnp.concatenate([lax.dot_general(perm, lo, _TN, preferred_element_type=F32),
                         lax.dot_general(perm, hi, _TN, preferred_element_type=F32)], axis=1)
    y_ref[...] = _rms(h_ref[...] + f, nf_ref[...])


def _largest_pow2_divisor(n, cap):
    t = cap
    while n % t:
        t //= 2
    return t


def _rotation_tables(pos, half):
    inv_freq = 1.0 / (ROPE_BASE ** jnp.linspace(0.0, 1.0, half, dtype=F32))
    ang = pos.astype(F32)[:, None] * inv_freq[None, :]
    cos, sin = jnp.cos(ang), jnp.sin(ang)
    return jnp.concatenate([cos, cos], axis=1), jnp.concatenate([-sin, sin], axis=1)


def _decay_tables(log_g, c):
    idx = jnp.arange(c, dtype=F32)
    lg = log_g[:, None]
    rel = idx[:, None] - idx[None, :]
    dec = jnp.where(rel >= 0, jnp.exp(lg[:, :, None] * jnp.maximum(rel, 0.0)), 0.0)
    rowdec = jnp.exp(lg * (idx + 1.0))[:, :, None]
    coldec = jnp.exp(lg * (c - 1.0 - idx))[:, :, None]
    gc = jnp.exp(lg * c)[:, :, None]
    return dec, rowdec, coldec, gc


def kernel(x_prompt, x_sample, cache_sb_k, cache_sb_v, state_ret, meta_tokens, norm_mix, w_in, w_branch_sb, w_branch_ret, w_out, norm_ffn, w_router, b_router, w_gate_up, b_gate_up, w_down, b_down, norm_final):
    B, SEQ, D = x_prompt.shape
    Bs, Ts, _ = x_sample.shape
    depth, _, past, h_sb, dh = cache_sb_k.shape
    _, _, h_ret, dk, dv = state_ret.shape
    n_meta = meta_tokens.shape[0]
    n_exp = w_router.shape[-1]
    d_ff = w_down.shape[-2]
    sbw, rqk, rv = h_sb * dh, h_ret * dk, h_ret * dv
    assert depth == 1, "one layer: the meta-token rows feed nothing after the mixers"
    assert dk == LANES and 2 * dh == LANES and h_sb * Ts == LANES and n_meta % 16 == 0

    TQ = _largest_pow2_divisor(SEQ, 512)
    assert TQ >= 2 * LANES and n_meta <= LANES
    t_p, t_s = B * SEQ, Bs * Ts
    t_moe = t_p + t_s
    TM = _largest_pow2_divisor(t_moe, 512)
    assert TM >= 16 and SEQ % TM == 0 and t_p % TM == 0 and t_moe % n_meta == 0 and t_p % Ts == 0
    n_p, n_moe = t_p // TM, t_moe // TM
    t_x = t_s + n_meta
    meta_blk = t_s // n_meta

    x_p = x_prompt.reshape(t_p, D)
    x_x = jnp.concatenate([x_sample.reshape(t_s, D), meta_tokens.astype(F32)], axis=0)

    sizes = (sbw,) * 3 + (rqk,) * 2 + (rv,) * 2 + (D,) * 2
    offs = tuple(sum(sizes[:i]) for i in range(len(sizes) + 1))
    n_sq = SEQ // TM
    row = lambda w: pl.BlockSpec((TM, w), lambda i: (i, 0))
    out_dt = (BF16, F32, F32, F32, F32, BF16, F32, F32, F32)
    w_in_b, g_mix = w_in[0].astype(BF16), norm_mix[0][None, :]

    def inproj(x, pos, tm, tab_period):
        rows = x.shape[0]
        blk = lambda w: pl.BlockSpec((tm, w), lambda i: (i, 0))
        tab = pl.BlockSpec((tm, dk), lambda i: (i % tab_period, 0))
        return pl.pallas_call(
            functools.partial(_inproj_body, offs=offs, q_scale=dh ** -0.5, k_scale=dk ** -0.5, n_ret=h_ret, dk=dk),
            grid=(rows // tm,),
            in_specs=[blk(D), pl.BlockSpec((1, D), lambda i: (0, 0)),
                      pl.BlockSpec((D, offs[-1]), lambda i: (0, 0), pipeline_mode=pl.Buffered(1)), tab, tab],
            out_specs=[blk(w) for w in sizes],
            out_shape=[jax.ShapeDtypeStruct((rows, w), dt) for w, dt in zip(sizes, out_dt)],
            compiler_params=_params("parallel"), name="inproj",
        )(x, g_mix, w_in_b, *_rotation_tables(pos, dk // 2))

    qa, ka, va, qr, kr, vr, gr, ga, gb = inproj(x_p, jnp.arange(SEQ), TM, n_sq)
    x_pos = jnp.concatenate([jnp.tile(past + jnp.arange(Ts), Bs), jnp.arange(n_meta) - n_meta])
    qa_x, ka_x, va_x, qr_x, kr_x, vr_x, gr_x, ga_x, gb_x = inproj(x_x, x_pos, t_x, 1)

    n_q = SEQ // TQ
    n_pair = sbw // LANES
    kv_full = pl.BlockSpec((1, n_meta + SEQ, LANES), lambda b, p, q: (b, 0, p))
    ya_p, k_full, v_full = pl.pallas_call(
        functools.partial(_sb_prompt_body, tq=TQ, n_meta=n_meta, dh=dh),
        grid=(B, n_pair, n_q),
        in_specs=[pl.BlockSpec((TQ, LANES), lambda b, p, q: (b * n_q + q, p)),
                  pl.BlockSpec((SEQ, LANES), lambda b, p, q: (b, p)),
                  pl.BlockSpec((SEQ, LANES), lambda b, p, q: (b, p)),
                  pl.BlockSpec((n_meta, LANES), lambda b, p, q: (meta_blk, p)),
                  pl.BlockSpec((n_meta, LANES), lambda b, p, q: (meta_blk, p))],
        out_specs=[pl.BlockSpec((TQ, LANES), lambda b, p, q: (b * n_q + q, p)), kv_full, kv_full],
        out_shape=[jax.ShapeDtypeStruct((t_p, sbw), BF16), jax.ShapeDtypeStruct((B, n_meta + SEQ, sbw), F32),
                   jax.ShapeDtypeStruct((B, n_meta + SEQ, sbw), F32)],
        scratch_shapes=[pltpu.VMEM((LANES + SEQ, LANES), BF16), pltpu.VMEM((LANES + SEQ, LANES), BF16),
                        pltpu.VMEM((2 * TQ, 1), F32), pltpu.VMEM((2 * TQ, LANES), F32)],
        compiler_params=_params("parallel", "parallel", "arbitrary"), name="sb_prompt",
    )(qa, ka, va, ka_x, va_x)

    KC = _largest_pow2_divisor(past, 1024)
    SUB = min(KC, 256)
    n_kc = past // KC
    smp = lambda w: pl.BlockSpec((Ts, w), lambda b, j: (b, 0))
    met = lambda w: pl.BlockSpec((n_meta, w), lambda b, j: (meta_blk, 0))
    cache = pl.BlockSpec((1, KC, sbw), lambda b, j: (b, n_kc - 1 - j, 0))
    ya_s = pl.pallas_call(
        functools.partial(_sb_sample_body, ts=Ts, n_meta=n_meta, dh=dh, sub=SUB),
        grid=(Bs, n_kc),
        in_specs=[smp(sbw), smp(sbw), smp(sbw), met(sbw), met(sbw), cache, cache],
        out_specs=pl.BlockSpec((Ts, sbw), lambda b, j: (b, 0)),
        out_shape=jax.ShapeDtypeStruct((t_s, sbw), BF16),
        scratch_shapes=[pltpu.VMEM((LANES, sbw), BF16), pltpu.VMEM((1, LANES), F32), pltpu.VMEM((LANES, sbw), F32)],
        compiler_params=_params("parallel", "arbitrary"), name="sb_sample",
    )(qa_x, ka_x, va_x, ka_x, va_x, cache_sb_k[0].reshape(Bs, past, sbw), cache_sb_v[0].reshape(Bs, past, sbw))

    log_g = jnp.log(1.0 - 2.0 ** (-5.0 - jnp.arange(h_ret, dtype=F32)))
    RC = _largest_pow2_divisor(SEQ, 256)
    dec, rd, cd, gc = _decay_tables(log_g, RC)
    _, _, mcd, _ = _decay_tables(log_g, n_meta)
    per_b = lambda w: pl.BlockSpec((SEQ, w), lambda b, h: (b, h))
    met_h = lambda w: pl.BlockSpec((n_meta, w), lambda b, h: (meta_blk, h))
    tab = lambda a: pl.BlockSpec((1,) + a.shape[1:], lambda b, h: (h, 0, 0))
    state_spec = pl.BlockSpec((1, 1, dk, dv), lambda b, h: (b, h, 0, 0))
    yr_p, s_p = pl.pallas_call(
        functools.partial(_ret_prompt_body, chunk=RC),
        grid=(B, h_ret),
        in_specs=[per_b(dk), per_b(dk), per_b(dv), per_b(dv), met_h(dk), met_h(dv),
                  tab(dec), tab(rd), tab(cd), tab(gc), tab(mcd)],
        out_specs=[per_b(dv), state_spec],
        out_shape=[jax.ShapeDtypeStruct((t_p, rv), BF16), jax.ShapeDtypeStruct((B, h_ret, dk, dv), F32)],
        compiler_params=_params("parallel", "parallel"), name="ret_prompt",
    )(qr, kr, vr, gr, kr_x, vr_x, dec, rd, cd, gc, mcd)

    dec_s, rd_s, cd_s, gc_s = _decay_tables(log_g, Ts)
    smp_h = lambda w: pl.BlockSpec((Ts, w), lambda b, h: (b, h))
    yr_s, s_s = pl.pallas_call(
        _ret_sample_body,
        grid=(Bs, h_ret),
        in_specs=[smp_h(dk), smp_h(dk), smp_h(dv), smp_h(dv), state_spec,
                  tab(dec_s), tab(rd_s), tab(cd_s), tab(gc_s)],
        out_specs=[smp_h(dv), state_spec],
        out_shape=[jax.ShapeDtypeStruct((t_s, rv), BF16), jax.ShapeDtypeStruct((Bs, h_ret, dk, dv), F32)],
        compiler_params=_params("parallel", "parallel"), name="ret_sample",
    )(qr_x, kr_x, vr_x, gr_x, state_ret[0], dec_s, rd_s, cd_s, gc_s)

    const = lambda a: pl.BlockSpec(a.shape, lambda i: (0,) * a.ndim)
    prm = lambda w: pl.BlockSpec((TM, w), lambda i: (jnp.minimum(i, n_p - 1), 0))
    smp_m = lambda w: pl.BlockSpec((TM, w), lambda i: (jnp.maximum(i - n_p, 0), 0))
    lane_major = pl.BlockSpec((TOP_K, TM), lambda i: (0, i))
    wa, wb, wo = w_branch_sb[0].astype(BF16), w_branch_ret[0].astype(BF16), w_out[0].astype(BF16)
    nf, wr_t, br = norm_ffn[0][None, :], w_router[0].T.astype(BF16), b_router[0][:, None]
    h1, tok, ids, gates, cnt = pl.pallas_call(
        functools.partial(_merge_body, n_prompt=n_p, top_k=TOP_K),
        grid=(n_moe,),
        in_specs=[prm(sbw), prm(rv), prm(D), prm(D), prm(D), smp_m(sbw), smp_m(rv), smp_m(D), smp_m(D), smp_m(D),
                  const(wa), const(wb), const(wo), const(nf), const(wr_t), const(br)],
        out_specs=[row(D), row(D // 2), lane_major, lane_major, pl.BlockSpec((1, n_exp, 1), lambda i: (i, 0, 0))],
        out_shape=[jax.ShapeDtypeStruct((t_moe, D), F32), jax.ShapeDtypeStruct((t_moe, D // 2), jnp.uint32),
                   jax.ShapeDtypeStruct((TOP_K, t_moe), jnp.int32), jax.ShapeDtypeStruct((TOP_K, t_moe), F32),
                   jax.ShapeDtypeStruct((n_moe, n_exp, 1), F32)],
        compiler_params=_params("parallel"), name="merge_router",
    )(ya_p, yr_p, ga, gb, x_p, ya_s, yr_s, ga_x, gb_x, x_x, wa, wb, wo, nf, wr_t, br)

    BM = TM
    cnt = cnt[:, :, 0].astype(jnp.int32)
    cnt = (cnt + SEG_ALIGN - 1) // SEG_ALIGN * SEG_ALIGN
    padded = (jnp.sum(cnt, axis=0) + BM - 1) // BM * BM
    pad_end = jnp.cumsum(padded)
    seg_row = (pad_end - padded)[None, :] + jnp.cumsum(cnt, axis=0) - cnt
    seg_loc = jnp.cumsum(cnt, axis=1) - cnt
    n_blk = -(-(t_moe * TOP_K + (SEG_ALIGN - 1) * n_moe * n_exp) // BM) + n_exp
    block_e = jnp.minimum(jnp.sum(pad_end[None, :] <= (jnp.arange(n_blk) * BM)[:, None], axis=1), n_exp - 1).astype(jnp.int32)
    n_used = (pad_end[-1:] // BM).astype(jnp.int32)
    seg_tabs = (cnt.reshape(-1), seg_loc.reshape(-1).astype(jnp.int32), seg_row.reshape(-1).astype(jnp.int32))
    grp_end = pad_end - padded + jnp.sum(cnt, axis=0)
    fill_rows = jnp.concatenate([grp_end, pad_end[-1:], pad_end - grp_end,
                                 (n_blk + 1) * BM - pad_end[-1:]]).astype(jnp.int32)

    XW = D // 2 + LANES
    n_loc = -(-(TOP_K * TM + SEG_ALIGN * n_exp) // LANES) * LANES
    xs, pos = pl.pallas_call(
        functools.partial(_dispatch_body, n_exp=n_exp),
        grid_spec=pltpu.PrefetchScalarGridSpec(
            num_scalar_prefetch=4, grid=(n_moe,),
            in_specs=[pl.BlockSpec((TOP_K, TM), lambda i, *_: (0, i)), pl.BlockSpec((TOP_K, TM), lambda i, *_: (0, i)),
                      pl.BlockSpec((1, n_exp, 1), lambda i, *_: (i, 0, 0)),
                      pl.BlockSpec((TM, D // 2), lambda i, *_: (i, 0))],
            out_specs=[pl.BlockSpec(memory_space=pl.ANY), pl.BlockSpec((TOP_K, TM), lambda i, *_: (0, i))],
            scratch_shapes=[pltpu.VMEM((2, n_loc, XW), jnp.uint32), pltpu.VMEM((SEG_BIG, XW), jnp.uint32),
                            pltpu.SemaphoreType.DMA((2,))]),
        out_shape=[jax.ShapeDtypeStruct(((n_blk + 1) * BM, XW), jnp.uint32),
                   jax.ShapeDtypeStruct((TOP_K, t_moe), jnp.int32)],
        compiler_params=_params("arbitrary"), name="moe_dispatch",
    )(*seg_tabs, fill_rows, ids, gates, seg_loc.astype(F32)[:, :, None], tok)

    blk = lambda i, be, nu: jnp.minimum(i, nu[0] - 1)
    exp_w = lambda a: pl.BlockSpec((1,) + a.shape[1:], lambda i, be, nu: (be[blk(i, be, nu)],) + (0,) * (a.ndim - 1))
    wgu, bgu = w_gate_up[0], b_gate_up[0][:, None, :]
    wd, bd = w_down[0], b_down[0][:, None, :]
    os_ = pl.pallas_call(
        functools.partial(_expert_body, d_ff=d_ff),
        grid_spec=pltpu.PrefetchScalarGridSpec(
            num_scalar_prefetch=2, grid=(n_blk,),
            in_specs=[pl.BlockSpec((BM, XW), lambda i, be, nu: (blk(i, be, nu), 0)),
                      exp_w(wgu), exp_w(bgu), exp_w(wd), exp_w(bd)],
            out_specs=pl.BlockSpec((BM, D // 2), lambda i, be, nu: (i, 0)),
            scratch_shapes=[pltpu.VMEM(wgu.shape[1:], BF16), pltpu.VMEM(wd.shape[1:], BF16)]),
        out_shape=jax.ShapeDtypeStruct((n_blk * BM, D // 2), jnp.uint32),
        compiler_params=_params("arbitrary"), name="moe_experts",
    )(block_e, n_used, xs, wgu, bgu, wd, bd)

    def combine(off, n):
        return pl.pallas_call(
            functools.partial(_combine_body, n_exp=n_exp, blk_off=off),
            grid_spec=pltpu.PrefetchScalarGridSpec(
                num_scalar_prefetch=3, grid=(n,),
                in_specs=[pl.BlockSpec((TOP_K, TM), lambda i, *_: (0, i + off)),
                          pl.BlockSpec((TM, D), lambda i, *_: (i + off, 0)),
                          pl.BlockSpec((1, D), lambda i, *_: (0, 0)), pl.BlockSpec(memory_space=pl.ANY)],
                out_specs=pl.BlockSpec((TM, D), lambda i, *_: (i, 0)),
                scratch_shapes=[pltpu.VMEM((2, n_loc, D // 2), jnp.uint32), pltpu.SemaphoreType.DMA((2,))]),
            out_shape=jax.ShapeDtypeStruct((n * TM, D), F32),
            compiler_params=_params("arbitrary"), name="moe_combine",
        )(*seg_tabs, pos, h1, norm_final[None, :], os_)

    y_prompt = combine(0, n_p).reshape(B, SEQ, D)
    y_sample = combine(n_p, n_moe - n_p).reshape(Bs, Ts, D)

    full = lambda a: a.reshape(1, B, n_meta + SEQ, h_sb, dh)
    return (y_prompt, y_sample, full(k_full), full(v_full), s_p[None],
            ka_x[:t_s].reshape(1, Bs, Ts, h_sb, dh), va_x[:t_s].reshape(1, Bs, Ts, h_sb, dh), s_s[None])
```
